```python
import jax, jax.numpy as jnp
from jax import lax

D_MODEL = 1024
BATCH = 2
SEQ = 16384
DEPTH = 2

GRID_W = 64
CTX_LEN = 256
N_MIXERS = 2
MIXER_NA = 0
MIXER_CONV = 1
N_HEADS = 16
HEAD_DIM = D_MODEL // N_HEADS
WIN_ROWS = 8
WIN_COLS = 16
CONV_WIDTH = 31
N_EXPERTS = 32
TOP_K = 4
D_EXPERT = D_MODEL
SWIGLU_LIMIT = 7.0
SWIGLU_ALPHA = 1.702
EXPERT_BLOCK = 256
NORM_EPS = 1e-6

kernel_name = "hybrid_natten_conformer_moe_diffusion_block"


def rms_norm(x, g):
    xf = x.astype(jnp.float32)
    y = xf * lax.rsqrt(jnp.mean(xf * xf, axis=-1, keepdims=True) + NORM_EPS)
    return (y * g.astype(jnp.float32)).astype(x.dtype)


def layer_norm(x, g, b):
    xf = x.astype(jnp.float32)
    mu = jnp.mean(xf, axis=-1, keepdims=True)
    var = jnp.mean(jnp.square(xf - mu), axis=-1, keepdims=True)
    y = (xf - mu) * lax.rsqrt(var + NORM_EPS)
    return (y * g.astype(jnp.float32) + b.astype(jnp.float32)).astype(x.dtype)


def modulate(h, shift, scale):
    return h * (1 + scale) + shift


def split_heads(t):
    return t.reshape(t.shape[:-1] + (N_HEADS, HEAD_DIM))


def context_feeds_later_layer(i):
    return any(j % N_MIXERS == MIXER_NA for j in range(i + 1, DEPTH))


def dense_attention(q, k, v):
    s = jnp.einsum('bqhd,bkhd->bhqk', q, k).astype(jnp.float32) * (HEAD_DIM ** -0.5)
    p = jax.nn.softmax(s, axis=-1).astype(v.dtype)
    o = jnp.einsum('bhqk,bkhd->bqhd', p, v)
    return o.reshape(o.shape[:2] + (D_MODEL,))


def neighbourhood_attention(q, k, v, k_ctx, v_ctx, rpb):
    b, s = q.shape[0], q.shape[1]
    rows = s // GRID_W
    wr = min(WIN_ROWS, rows)
    n_lat = wr * WIN_COLS
    grid = lambda t: t.reshape(b, rows, GRID_W, N_HEADS, HEAD_DIM)
    qg, kg, vg = grid(q), grid(k), grid(v)
    cols = jnp.arange(GRID_W)
    col_idx = jnp.clip(cols - WIN_COLS // 2, 0, GRID_W - WIN_COLS)[:, None] + jnp.arange(WIN_COLS)[None, :]
    col_off = col_idx - cols[:, None] + (WIN_COLS - 1)
    rpb_f = rpb.astype(jnp.float32)
    scale = HEAD_DIM ** -0.5

    def row_block(r):
        rs = jnp.clip(r - WIN_ROWS // 2, 0, rows - wr)
        qb = lax.dynamic_index_in_dim(qg, r, axis=1, keepdims=False)

        def gather_kv(t):
            t = lax.dynamic_slice_in_dim(t, rs, wr, axis=1)[:, :, col_idx]
            return jnp.moveaxis(t, 2, 1).reshape(b, GRID_W, n_lat, N_HEADS, HEAD_DIM)

        kb, vb = gather_kv(kg), gather_kv(vg)
        row_off = rs + jnp.arange(wr) - r + (WIN_ROWS - 1)
        bias = rpb_f[:, row_off[None, :, None], col_off[:, None, :]].reshape(N_HEADS, GRID_W, n_lat)
        s_lat = jnp.einsum('bqhd,bqkhd->bhqk', qb, kb).astype(jnp.float32) * scale + bias
        s_ctx = jnp.einsum('bqhd,bkhd->bhqk', qb, k_ctx).astype(jnp.float32) * scale
        p = jax.nn.softmax(jnp.concatenate([s_lat, s_ctx], axis=-1), axis=-1).astype(v.dtype)
        return (jnp.einsum('bhqk,bqkhd->bqhd', p[..., :n_lat], vb)
                + jnp.einsum('bhqk,bkhd->bqhd', p[..., n_lat:], v_ctx))

    o = lax.map(row_block, jnp.arange(rows))
    return jnp.moveaxis(o, 0, 1).reshape(b, s, D_MODEL)


def conformer_conv(h, w_pw1, b_pw1, w_dw, b_dw, ln_g, ln_b, w_pw2, b_pw2):
    a = h @ w_pw1 + b_pw1
    u = a[..., :D_MODEL] * jax.nn.sigmoid(a[..., D_MODEL:])
    u = lax.conv_general_dilated(u, w_dw[:, None, :].astype(u.dtype), (1,), [(CONV_WIDTH // 2, CONV_WIDTH // 2)],
                                 dimension_numbers=('NWC', 'WIO', 'NWC'), feature_group_count=D_MODEL) + b_dw
    u = jax.nn.silu(layer_norm(u, ln_g, ln_b))
    return u @ w_pw2 + b_pw2


def clamped_swiglu_expert(xb, w_gu, b_gu, w_down, b_down):
    gu = xb @ w_gu + b_gu
    gate = jnp.minimum(gu[..., :D_EXPERT], SWIGLU_LIMIT)
    up = jnp.clip(gu[..., D_EXPERT:], -SWIGLU_LIMIT, SWIGLU_LIMIT)
    glu = gate * jax.nn.sigmoid(gate * SWIGLU_ALPHA)
    return ((up + 1) * glu) @ w_down + b_down


def moe(h, w_router, b_router, w_gu, b_gu, w_down, b_down):
    shape = h.shape
    x = h.reshape(-1, D_MODEL)
    n_tok = x.shape[0]
    logits = (x @ w_router + b_router).astype(jnp.float32)
    top_val, top_idx = lax.top_k(logits, TOP_K)
    gates = jax.nn.softmax(top_val, axis=-1)
    n_assign = n_tok * TOP_K
    flat_e = top_idx.reshape(-1)
    flat_tok = jnp.repeat(jnp.arange(n_tok), TOP_K)
    order = jnp.argsort(flat_e, stable=True)
    se, stok, sg = flat_e[order], flat_tok[order], gates.reshape(-1)[order]
    counts = jnp.bincount(flat_e, length=N_EXPERTS)
    start = jnp.cumsum(counts) - counts
    pcounts = (counts + EXPERT_BLOCK - 1) // EXPERT_BLOCK * EXPERT_BLOCK
    pend = jnp.cumsum(pcounts)
    pstart = pend - pcounts
    dest = pstart[se] + (jnp.arange(n_assign) - start[se])
    n_blocks = -(-n_assign // EXPERT_BLOCK) + N_EXPERTS
    n_rows = n_blocks * EXPERT_BLOCK
    row_tok = jnp.full((n_rows,), n_tok, jnp.int32).at[dest].set(stok)
    row_gate = jnp.zeros((n_rows,), x.dtype).at[dest].set(sg.astype(x.dtype))
    blk_e = jnp.minimum(jnp.searchsorted(pend, jnp.arange(n_blocks) * EXPERT_BLOCK, side='right'), N_EXPERTS - 1)
    x_pad = jnp.concatenate([x, jnp.zeros((1, D_MODEL), x.dtype)], axis=0)

    def expert_block(args):
        tok, e = args
        return clamped_swiglu_expert(x_pad[tok], w_gu[e], b_gu[e], w_down[e], b_down[e])

    y = lax.map(expert_block, (row_tok.reshape(n_blocks, EXPERT_BLOCK), blk_e))
    y = y.reshape(n_rows, D_MODEL) * row_gate[:, None]
    out = jnp.zeros((n_tok + 1, D_MODEL), y.dtype).at[row_tok].add(y)[:n_tok]
    return out.reshape(shape)


def setup_inputs(seed: int = 0) -> dict:
    key = jax.random.key(seed)
    ks = jax.random.split(key, 24)
    d, e, f = D_MODEL, N_EXPERTS, D_EXPERT
    n_na = len(range(MIXER_NA, DEPTH, N_MIXERS))
    n_conv = len(range(MIXER_CONV, DEPTH, N_MIXERS))
    nrm = lambda k, shp, s: jax.random.normal(k, shp, jnp.float32) * s
    return {
        'x': nrm(ks[0], (BATCH, SEQ, d), 1.0),
        'c': nrm(ks[1], (BATCH, d), 1.0),
        'ctx': nrm(ks[2], (BATCH, CTX_LEN, d), 1.0),
        'c_ctx': nrm(ks[3], (d,), 1.0),
        'w_ada': nrm(ks[4], (DEPTH, d, 6 * d), 0.5 * d ** -0.5),
        'b_ada': nrm(ks[5], (DEPTH, 6 * d), 0.01),
        'norm_g': 1.0 + nrm(ks[6], (DEPTH, 4, d), 0.1),
        'w_qkv': nrm(ks[7], (n_na, d, 3 * d), d ** -0.5),
        'w_o': nrm(ks[8], (n_na, d, d), d ** -0.5),
        'rpb': nrm(ks[9], (n_na, N_HEADS, 2 * WIN_ROWS - 1, 2 * WIN_COLS - 1), 0.1),
        'w_pw1': nrm(ks[10], (n_conv, d, 2 * d), d ** -0.5),
        'b_pw1': nrm(ks[11], (n_conv, 2 * d), 0.01),
        'w_dw': nrm(ks[12], (n_conv, CONV_WIDTH, d), CONV_WIDTH ** -0.5),
        'b_dw': nrm(ks[13], (n_conv, d), 0.01),
        'cn_g': 1.0 + nrm(ks[14], (n_conv, d), 0.1),
        'cn_b': nrm(ks[15], (n_conv, d), 0.01),
        'w_pw2': nrm(ks[16], (n_conv, d, d), d ** -0.5),
        'b_pw2': nrm(ks[17], (n_conv, d), 0.01),
        'w_router': nrm(ks[18], (DEPTH, d, e), d ** -0.5),
        'b_router': nrm(ks[19], (DEPTH, e), 0.01),
        'w_gu': nrm(ks[20], (DEPTH, e, d, 2 * f), d ** -0.5),
        'b_gu': nrm(ks[21], (DEPTH, e, 2 * f), 0.01),
        'w_down': nrm(ks[22], (DEPTH, e, f, d), f ** -0.5),
        'b_down': nrm(ks[23], (DEPTH, e, d), 0.01),
    }


def reference(x, c, ctx, c_ctx, w_ada, b_ada, norm_g, w_qkv, w_o, rpb, w_pw1, b_pw1, w_dw, b_dw,
              cn_g, cn_b, w_pw2, b_pw2, w_router, b_router, w_gu, b_gu, w_down, b_down):
    silu_c = jax.nn.silu(c)
    silu_cc = jax.nn.silu(c_ctx)
    for i in range(DEPTH):
        kind, j = i % N_MIXERS, i // N_MIXERS
        ctx_live = context_feeds_later_layer(i)
        sh1, sc1, g1, sh2, sc2, g2 = jnp.split((silu_c @ w_ada[i] + b_ada[i])[:, None, :], 6, axis=-1)
        if kind == MIXER_NA or ctx_live:
            csh1, csc1, cg1, csh2, csc2, cg2 = jnp.split(silu_cc @ w_ada[i] + b_ada[i], 6)
            hc = modulate(rms_norm(ctx, norm_g[i, 0]), csh1, csc1)
        h = modulate(rms_norm(x, norm_g[i, 0]), sh1, sc1)
        if kind == MIXER_NA:
            w = w_qkv[j]
            q, k, v = [split_heads(t) for t in jnp.split(h @ w, 3, axis=-1)]
            kc, vc = [split_heads(t) for t in jnp.split(hc @ w[:, D_MODEL:], 2, axis=-1)]
            y = neighbourhood_attention(q, k, v, kc, vc, rpb[j]) @ w_o[j]
            if ctx_live:
                yc = dense_attention(split_heads(hc @ w[:, :D_MODEL]), kc, vc) @ w_o[j]
        else:
            conv_w = (w_pw1[j], b_pw1[j], w_dw[j], b_dw[j], cn_g[j], cn_b[j], w_pw2[j], b_pw2[j])
            y = conformer_conv(h, *conv_w)
            if ctx_live:
                yc = conformer_conv(hc, *conv_w)
        moe_w = (w_router[i], b_router[i], w_gu[i], b_gu[i], w_down[i], b_down[i])
        x = x + g1 * rms_norm(y, norm_g[i, 1])
        x = x + g2 * rms_norm(moe(modulate(rms_norm(x, norm_g[i, 2]), sh2, sc2), *moe_w), norm_g[i, 3])
        if ctx_live:
            ctx = ctx + cg1 * rms_norm(yc, norm_g[i, 1])
            ctx = ctx + cg2 * rms_norm(moe(modulate(rms_norm(ctx, norm_g[i, 2]), csh2, csc2), *moe_w), norm_g[i, 3])
    return x
```

```python
import functools

import jax
import jax.numpy as jnp
from jax import lax
from jax.experimental import pallas as pl
from jax.experimental.pallas import tpu as pltpu

D = 1024
GRID_W = 64
N_HEADS = 16
HEAD_DIM = 64
HEAD_PAIRS = N_HEADS // 2
WIN_ROWS = 8
WIN_COLS = 16
CONV_W = 31
CONV_HALO = 16
N_EXPERTS = 32
TOP_K = 4
SWIGLU_LIMIT = 7.0
SWIGLU_ALPHA = 1.702
EXPERT_BLOCK = 256
EPS = 1e-6
NEG = -1e30
NA_CHUNK_ROWS = 8
V7X_VMEM_LIMIT = 56 * 1024 * 1024

f32 = jnp.float32
bf16 = jnp.bfloat16


def _params(*sem):
    return pltpu.CompilerParams(dimension_semantics=sem, vmem_limit_bytes=V7X_VMEM_LIMIT)


def _rms(v, g):
    return v * lax.rsqrt(jnp.mean(v * v, axis=-1, keepdims=True) + EPS) * g


def _ada_kernel(c_ref, w_ref, b_ref, o_ref):
    c = c_ref[...]
    s = c * jax.nn.sigmoid(c)
    o_ref[...] = jnp.dot(s, w_ref[...], preferred_element_type=f32,
                         precision=lax.Precision.HIGHEST) + b_ref[...]


def _ada(cvec, w_ada, b_ada):
    depth, _, n = w_ada.shape
    tn = 1536
    return pl.pallas_call(
        _ada_kernel,
        grid=(depth, n // tn),
        in_specs=[pl.BlockSpec((8, D), lambda i, j: (0, 0)),
                  pl.BlockSpec((None, D, tn), lambda i, j: (i, 0, j)),
                  pl.BlockSpec((None, 1, tn), lambda i, j: (i, 0, j))],
        out_specs=pl.BlockSpec((None, 8, tn), lambda i, j: (i, 0, j)),
        out_shape=jax.ShapeDtypeStruct((depth, 8, n), f32),
        compiler_params=_params("arbitrary", "arbitrary"),
        name="ada",
    )(cvec, w_ada, b_ada.reshape(depth, 1, n))


def _prenorm_matmul_kernel(x_ref, g_ref, sh_ref, sc_ref, w_ref, b_ref, o_ref, *, mode, tn):
    h = _rms(x_ref[...], g_ref[...]) * (1.0 + sc_ref[...]) + sh_ref[...]
    hb = h.astype(bf16)
    n_out = o_ref.shape[-1]
    for j in range(n_out // tn):
        if mode == "glu":
            a = jnp.dot(hb, w_ref[:, j * tn:(j + 1) * tn], preferred_element_type=f32) + b_ref[:, j * tn:(j + 1) * tn]
            g = (jnp.dot(hb, w_ref[:, n_out + j * tn:n_out + (j + 1) * tn], preferred_element_type=f32)
                 + b_ref[:, n_out + j * tn:n_out + (j + 1) * tn])
            o_ref[:, j * tn:(j + 1) * tn] = (a * jax.nn.sigmoid(g)).astype(o_ref.dtype)
        else:
            a = jnp.dot(hb, w_ref[:, j * tn:(j + 1) * tn], preferred_element_type=f32)
            if mode == "qkv" and (j + 1) * tn <= D:
                a = a * (HEAD_DIM ** -0.5)
            o_ref[:, j * tn:(j + 1) * tn] = a.astype(o_ref.dtype)


def _prenorm_matmul(x, g, sh, sc, w, b, *, mode, tm, out_dtype):
    bsz, seq, _ = x.shape
    n = w.shape[1]
    n_out = n // 2 if mode == "glu" else n
    tn = 512
    kern = functools.partial(_prenorm_matmul_kernel, mode=mode, tn=tn)
    return pl.pallas_call(
        kern,
        grid=(bsz, seq // tm),
        in_specs=[pl.BlockSpec((None, tm, D), lambda b_, t: (b_, t, 0)),
                  pl.BlockSpec((1, D), lambda b_, t: (0, 0)),
                  pl.BlockSpec((None, 1, D), lambda b_, t: (b_, 0, 0)),
                  pl.BlockSpec((None, 1, D), lambda b_, t: (b_, 0, 0)),
                  pl.BlockSpec((D, n), lambda b_, t: (0, 0)),
                  pl.BlockSpec((1, n), lambda b_, t: (0, 0))],
        out_specs=pl.BlockSpec((None, tm, n_out), lambda b_, t: (b_, t, 0)),
        out_shape=jax.ShapeDtypeStruct((bsz, seq, n_out), out_dtype),
        compiler_params=_params("parallel", "parallel"),
        name="prenorm_matmul_" + mode,
    )(x, g, sh, sc, w, b)


def _na_kernel(q_ref, kp_ref, kc_ref, kn_ref, vp_ref, vc_ref, vn_ref, kx_ref, vx_ref, bias_ref, o_ref,
               k_buf, v_buf, *, rows):
    c = pl.program_id(2)
    blk = NA_CHUNK_ROWS * GRID_W
    k_buf[0:blk] = kp_ref[...]
    k_buf[blk:2 * blk] = kc_ref[...]
    k_buf[2 * blk:3 * blk] = kn_ref[...]
    v_buf[0:blk] = vp_ref[...]
    v_buf[blk:2 * blk] = vc_ref[...]
    v_buf[2 * blk:3 * blk] = vn_ref[...]
    lane = lax.broadcasted_iota(jnp.int32, (GRID_W, 2 * HEAD_DIM), 1)
    first_head = lane < HEAD_DIM
    nt = (((1,), (1,)), ((), ()))
    tn = (((0,), (0,)), ((), ()))
    kx = kx_ref[...]
    vx = vx_ref[...]
    win = WIN_ROWS * GRID_W

    def body(i, carry):
        r = c * NA_CHUNK_ROWS + i
        rs = jnp.clip(r - WIN_ROWS // 2, 0, rows - WIN_ROWS)
        delta = r - rs
        start = pl.multiple_of((rs - (c - 1) * NA_CHUNK_ROWS) * GRID_W, GRID_W)
        q = q_ref[pl.ds(pl.multiple_of(i * GRID_W, GRID_W), GRID_W), :]
        zero = jnp.zeros_like(q)
        qbd = jnp.concatenate([jnp.where(first_head, q, zero), jnp.where(first_head, zero, q)], axis=0)
        kw = k_buf[pl.ds(start, win), :]
        s_lat = lax.dot_general(kw, qbd, nt, preferred_element_type=f32) + bias_ref[delta]
        s_ctx = lax.dot_general(kx, qbd, nt, preferred_element_type=f32)
        m = jnp.maximum(jnp.max(s_lat, axis=0, keepdims=True), jnp.max(s_ctx, axis=0, keepdims=True))
        p_lat = jnp.exp(s_lat - m)
        p_ctx = jnp.exp(s_ctx - m)
        inv = 1.0 / (jnp.sum(p_lat, axis=0, keepdims=True) + jnp.sum(p_ctx, axis=0, keepdims=True))
        p_lat = (p_lat * inv).astype(bf16)
        p_ctx = (p_ctx * inv).astype(bf16)
        vw = v_buf[pl.ds(start, win), :]
        o = (lax.dot_general(p_lat, vw, tn, preferred_element_type=f32)
             + lax.dot_general(p_ctx, vx, tn, preferred_element_type=f32))
        o_ref[pl.ds(pl.multiple_of(i * GRID_W, GRID_W), GRID_W), :] = jnp.where(
            first_head, o[0:GRID_W], o[GRID_W:2 * GRID_W]).astype(o_ref.dtype)
        return carry

    lax.fori_loop(0, NA_CHUNK_ROWS, body, 0)


def _na_bias_table(rpb):
    qc = jnp.arange(GRID_W)
    cs = jnp.clip(qc - WIN_COLS // 2, 0, GRID_W - WIN_COLS)
    kc = jnp.arange(GRID_W)
    inside = (kc[None, :] >= cs[:, None]) & (kc[None, :] < cs[:, None] + WIN_COLS)
    col_off = jnp.clip(kc[None, :] - qc[:, None] + (WIN_COLS - 1), 0, 2 * WIN_COLS - 2)
    delta = jnp.arange(WIN_ROWS)
    j = jnp.arange(WIN_ROWS)
    row_off = j[None, :] - delta[:, None] + (WIN_ROWS - 1)
    t = rpb.astype(f32)[:, row_off[:, :, None, None], col_off[None, None, :, :]]
    t = jnp.where(inside[None, None, None], t, NEG)
    t = jnp.transpose(t, (1, 2, 4, 0, 3))
    t = t.reshape(WIN_ROWS, WIN_ROWS * GRID_W, HEAD_PAIRS, 2 * GRID_W)
    return jnp.transpose(t, (2, 0, 1, 3))


def _na_attention(qkv, kv_ctx, bias_t):
    bsz, seq, _ = qkv.shape
    ctx_len = kv_ctx.shape[1]
    rows = seq // GRID_W
    blk = NA_CHUNK_ROWS * GRID_W
    n_chunks = seq // blk
    hp2 = 2 * HEAD_DIM
    kblk = D // hp2
    vblk = 2 * D // hp2

    def spec(col0, shift):
        return pl.BlockSpec((None, blk, hp2),
                            lambda b, h, c: (b, jnp.clip(c + shift, 0, n_chunks - 1), col0 + h))

    return pl.pallas_call(
        functools.partial(_na_kernel, rows=rows),
        grid=(bsz, HEAD_PAIRS, n_chunks),
        in_specs=[spec(0, 0),
                  spec(kblk, -1), spec(kblk, 0), spec(kblk, 1),
                  spec(vblk, -1), spec(vblk, 0), spec(vblk, 1),
                  pl.BlockSpec((None, ctx_len, hp2), lambda b, h, c: (b, 0, kblk + h)),
                  pl.BlockSpec((None, ctx_len, hp2), lambda b, h, c: (b, 0, vblk + h)),
                  pl.BlockSpec((None, WIN_ROWS, WIN_ROWS * GRID_W, hp2), lambda b, h, c: (h, 0, 0, 0))],
        out_specs=pl.BlockSpec((None, blk, hp2), lambda b, h, c: (b, c, h)),
        out_shape=jax.ShapeDtypeStruct((bsz, seq, D), bf16),
        scratch_shapes=[pltpu.VMEM((3 * blk, hp2), bf16), pltpu.VMEM((3 * blk, hp2), bf16)],
        compiler_params=_params("parallel", "parallel", "arbitrary"),
        name="na_attention",
    )(qkv, qkv, qkv, qkv, qkv, qkv, qkv, kv_ctx, kv_ctx, bias_t)


def _mixer_epilogue(y, x_ref, gate_ref, g_post_ref, g_pre_ref, sh_ref, sc_ref, wr_ref, br_ref,
                    x1_ref, h2_ref, ti_ref, tg_ref):
    x1 = x_ref[...] + gate_ref[...] * _rms(y, g_post_ref[...])
    x1_ref[...] = x1
    h2 = _rms(x1, g_pre_ref[...]) * (1.0 + sc_ref[...]) + sh_ref[...]
    hb = h2.astype(bf16)
    h2_ref[...] = hb
    logits = jnp.dot(hb, wr_ref[...], preferred_element_type=f32) + br_ref[...]
    eidx = lax.broadcasted_iota(jnp.int32, logits.shape, 1).astype(f32)
    work = logits
    vals, idxs = [], []
    for _ in range(TOP_K):
        m = jnp.max(work, axis=-1, keepdims=True)
        first = jnp.min(jnp.where(work == m, eidx, float(N_EXPERTS)), axis=-1, keepdims=True)
        vals.append(m)
        idxs.append(first)
        work = jnp.where(eidx == first, NEG, work)
    top_v = jnp.concatenate(vals, axis=-1)
    e = jnp.exp(top_v - vals[0])
    tg_ref[...] = e / jnp.sum(e, axis=-1, keepdims=True)
    ti_ref[...] = jnp.concatenate(idxs, axis=-1).astype(jnp.int32)


def _attn_out_kernel(a_ref, wo_ref, *rest):
    y = jnp.dot(a_ref[...], wo_ref[...], preferred_element_type=f32)
    _mixer_epilogue(y, *rest)


def _epilogue_specs(tm):
    tok = lambda b, t: (b, t, 0)
    per_b = lambda b, t: (b, 0, 0)
    const = lambda b, t: (0, 0)
    in_specs = [pl.BlockSpec((None, tm, D), tok),
                pl.BlockSpec((None, 1, D), per_b),
                pl.BlockSpec((1, D), const),
                pl.BlockSpec((1, D), const),
                pl.BlockSpec((None, 1, D), per_b),
                pl.BlockSpec((None, 1, D), per_b),
                pl.BlockSpec((D, N_EXPERTS), const),
                pl.BlockSpec((1, N_EXPERTS), const)]
    out_specs = [pl.BlockSpec((None, tm, D), tok),
                 pl.BlockSpec((None, tm, D), tok),
                 pl.BlockSpec((None, tm, TOP_K), tok),
                 pl.BlockSpec((None, tm, TOP_K), tok)]
    return in_specs, out_specs


def _epilogue_shapes(bsz, seq):
    return [jax.ShapeDtypeStruct((bsz, seq, D), f32),
            jax.ShapeDtypeStruct((bsz, seq, D), bf16),
            jax.ShapeDtypeStruct((bsz, seq, TOP_K), jnp.int32),
            jax.ShapeDtypeStruct((bsz, seq, TOP_K), f32)]


def _attn_out(a, wo, epi_args, *, tm):
    bsz, seq, _ = a.shape
    in_specs, out_specs = _epilogue_specs(tm)
    return pl.pallas_call(
        _attn_out_kernel,
        grid=(bsz, seq // tm),
        in_specs=[pl.BlockSpec((None, tm, D), lambda b, t: (b, t, 0)),
                  pl.BlockSpec((D, D), lambda b, t: (0, 0))] + in_specs,
        out_specs=out_specs,
        out_shape=_epilogue_shapes(bsz, seq),
        compiler_params=_params("parallel", "parallel"),
        name="attn_out",
    )(a, wo, *epi_args)


def _conv_kernel(up_ref, uc_ref, un_ref, wdw_ref, bdw_ref, lg_ref, lb_ref, w2_ref, b2_ref, *rest, tm, sub):
    ext, = rest[-1:]
    t = pl.program_id(1)
    nt = pl.num_programs(1)
    ext[0:CONV_HALO] = jnp.where(t > 0, up_ref[...], 0.0)
    ext[CONV_HALO:CONV_HALO + tm] = uc_ref[...]
    ext[CONV_HALO + tm:] = jnp.where(t < nt - 1, un_ref[...], 0.0)
    base = CONV_HALO - CONV_W // 2
    parts = []
    for rc in range(tm // sub):
        acc = jnp.zeros((sub, D), f32) + bdw_ref[...]
        for k in range(CONV_W):
            acc = acc + ext[rc * sub + base + k:rc * sub + base + k + sub, :] * wdw_ref[k:k + 1, :]
        mu = jnp.mean(acc, axis=-1, keepdims=True)
        d = acc - mu
        var = jnp.mean(d * d, axis=-1, keepdims=True)
        z = d * lax.rsqrt(var + EPS) * lg_ref[...] + lb_ref[...]
        parts.append((z * jax.nn.sigmoid(z)).astype(bf16))
    z = jnp.concatenate(parts, axis=0)
    y = jnp.dot(z, w2_ref[...], preferred_element_type=f32) + b2_ref[...]
    _mixer_epilogue(y, *rest[:-1])


def _conv_tail(u, w_dw, b_dw, ln_g, ln_b, w2, b2, epi_args, *, tm):
    bsz, seq, _ = u.shape
    in_specs, out_specs = _epilogue_specs(tm)
    per = tm // CONV_HALO
    n_halo = seq // CONV_HALO
    const = lambda b, t: (0, 0)
    return pl.pallas_call(
        functools.partial(_conv_kernel, tm=tm, sub=32),
        grid=(bsz, seq // tm),
        in_specs=[pl.BlockSpec((None, CONV_HALO, D), lambda b, t: (b, jnp.maximum(t * per - 1, 0), 0)),
                  pl.BlockSpec((None, tm, D), lambda b, t: (b, t, 0)),
                  pl.BlockSpec((None, CONV_HALO, D), lambda b, t: (b, jnp.minimum((t + 1) * per, n_halo - 1), 0)),
                  pl.BlockSpec((CONV_W, D), const),
                  pl.BlockSpec((1, D), const),
                  pl.BlockSpec((1, D), const),
                  pl.BlockSpec((1, D), const),
                  pl.BlockSpec((D, D), const),
                  pl.BlockSpec((1, D), const)] + in_specs,
        out_specs=out_specs,
        out_shape=_epilogue_shapes(bsz, seq),
        scratch_shapes=[pltpu.VMEM((tm + 2 * CONV_HALO, D), f32)],
        compiler_params=_params("parallel", "arbitrary"),
        name="conv_tail",
    )(u, u, u, w_dw, b_dw, ln_g, ln_b, w2, b2, *epi_args)


def _experts_kernel(be_ref, bv_ref, xs_ref, wgu_ref, bgu_ref, wd_ref, bd_ref, y_ref):
    b = pl.program_id(0)

    @pl.when(bv_ref[b] == 1)
    def _():
        gu = jnp.dot(xs_ref[...], wgu_ref[...], preferred_element_type=f32) + bgu_ref[...]
        gate = jnp.minimum(gu[:, :D], SWIGLU_LIMIT)
        up = jnp.clip(gu[:, D:], -SWIGLU_LIMIT, SWIGLU_LIMIT)
        glu = gate * jax.nn.sigmoid(gate * SWIGLU_ALPHA)
        a = ((up + 1.0) * glu).astype(bf16)
        y_ref[...] = jnp.dot(a, wd_ref[...], preferred_element_type=f32) + bd_ref[...]

    @pl.when(bv_ref[b] == 0)
    def _():
        y_ref[...] = jnp.zeros_like(y_ref)


def _experts(xs, blk_e, blk_valid, w_gu, b_gu, w_down, b_down):
    n_rows = xs.shape[0]
    n_blocks = n_rows // EXPERT_BLOCK
    grid_spec = pltpu.PrefetchScalarGridSpec(
        num_scalar_prefetch=2,
        grid=(n_blocks,),
        in_specs=[pl.BlockSpec((EXPERT_BLOCK, D), lambda b, be, bv: (b, 0)),
                  pl.BlockSpec((None, D, 2 * D), lambda b, be, bv: (be[b], 0, 0)),
                  pl.BlockSpec((None, 1, 2 * D), lambda b, be, bv: (be[b], 0, 0)),
                  pl.BlockSpec((None, D, D), lambda b, be, bv: (be[b], 0, 0)),
                  pl.BlockSpec((None, 1, D), lambda b, be, bv: (be[b], 0, 0))],
        out_specs=pl.BlockSpec((EXPERT_BLOCK, D), lambda b, be, bv: (b, 0)),
    )
    return pl.pallas_call(
        _experts_kernel,
        grid_spec=grid_spec,
        out_shape=jax.ShapeDtypeStruct((n_rows, D), f32),
        compiler_params=_params("arbitrary"),
        name="experts",
    )(blk_e, blk_valid, xs, w_gu, b_gu.reshape(N_EXPERTS, 1, 2 * D), w_down, b_down.reshape(N_EXPERTS, 1, D))


def _moe(h2, top_i, top_g, w_gu, b_gu, w_down, b_down):
    n_tok = h2.shape[0]
    n_assign = n_tok * TOP_K
    onehot = (top_i[:, :, None] == jnp.arange(N_EXPERTS)[None, None, :]).any(axis=1).astype(jnp.int32)
    rank = jnp.cumsum(onehot, axis=0) - onehot
    counts = jnp.sum(onehot, axis=0)
    pcounts = (counts + EXPERT_BLOCK - 1) // EXPERT_BLOCK * EXPERT_BLOCK
    pend = jnp.cumsum(pcounts)
    pstart = pend - pcounts
    dest = pstart[top_i] + jnp.take_along_axis(rank, top_i, axis=1)
    n_blocks = -(-n_assign // EXPERT_BLOCK) + N_EXPERTS
    n_rows = n_blocks * EXPERT_BLOCK
    row_tok = jnp.full((n_rows,), n_tok, jnp.int32).at[dest.reshape(-1)].set(
        jnp.repeat(jnp.arange(n_tok, dtype=jnp.int32), TOP_K))
    blk_start = jnp.arange(n_blocks, dtype=jnp.int32) * EXPERT_BLOCK
    blk_e = jnp.minimum(jnp.searchsorted(pend, blk_start, side='right'), N_EXPERTS - 1).astype(jnp.int32)
    blk_valid = (blk_start < pend[-1]).astype(jnp.int32)
    x_pad = jnp.concatenate([h2, jnp.zeros((1, D), h2.dtype)], axis=0)
    xs = x_pad[row_tok]
    y = _experts(xs, blk_e, blk_valid, w_gu, b_gu, w_down, b_down)
    return jnp.sum(y[dest] * top_g[:, :, None], axis=1)


def _moe_out_kernel(x_ref, m_ref, gate_ref, g_ref, o_ref):
    o_ref[...] = x_ref[...] + gate_ref[...] * _rms(m_ref[...], g_ref[...])


def _moe_out(x1, m, gate, g, *, tm):
    bsz, seq, _ = x1.shape
    tok = lambda b, t: (b, t, 0)
    return pl.pallas_call(
        _moe_out_kernel,
        grid=(bsz, seq // tm),
        in_specs=[pl.BlockSpec((None, tm, D), tok),
                  pl.BlockSpec((None, tm, D), tok),
                  pl.BlockSpec((None, 1, D), lambda b, t: (b, 0, 0)),
                  pl.BlockSpec((1, D), lambda b, t: (0, 0))],
        out_specs=pl.BlockSpec((None, tm, D), tok),
        out_shape=jax.ShapeDtypeStruct((bsz, seq, D), f32),
        compiler_params=_params("parallel", "parallel"),
        name="moe_out",
    )(x1, m, gate, g)


def kernel(x, c, ctx, c_ctx, w_ada, b_ada, norm_g, w_qkv, w_o, rpb, w_pw1, b_pw1, w_dw, b_dw, cn_g, cn_b,
           w_pw2, b_pw2, w_router, b_router, w_gu, b_gu, w_down, b_down):
    bsz, seq, _ = x.shape
    tm = min(512, seq)
    cvec = jnp.zeros((8, D), f32).at[:bsz].set(c).at[bsz].set(c_ctx)
    mod = _ada(cvec, w_ada, b_ada)

    def lat(i, k):
        return mod[i, :bsz, k * D:(k + 1) * D][:, None, :]

    def cx(i, k):
        return jnp.broadcast_to(mod[i, bsz, k * D:(k + 1) * D][None, None, :], (bsz, 1, D))

    row = lambda v: v.reshape(1, -1)
    zero_b = jnp.zeros((1, 3 * D), f32)

    wq = w_qkv[0].astype(bf16)
    qkv = _prenorm_matmul(x, row(norm_g[0, 0]), lat(0, 0), lat(0, 1), wq, zero_b,
                          mode="qkv", tm=tm, out_dtype=bf16)
    kv_ctx = _prenorm_matmul(ctx, row(norm_g[0, 0]), cx(0, 0), cx(0, 1), wq, zero_b,
                             mode="qkv", tm=ctx.shape[1], out_dtype=bf16)
    attn = _na_attention(qkv, kv_ctx, _na_bias_table(rpb[0]))
    epi = (x, lat(0, 2), row(norm_g[0, 1]), row(norm_g[0, 2]), lat(0, 3), lat(0, 4),
           w_router[0].astype(bf16), row(b_router[0]))
    x1, h2, ti, tg = _attn_out(attn, w_o[0].astype(bf16), epi, tm=tm)
    m = _moe(h2.reshape(-1, D), ti.reshape(-1, TOP_K), tg.reshape(-1, TOP_K),
             w_gu[0].astype(bf16), b_gu[0], w_down[0].astype(bf16), b_down[0])
    x = _moe_out(x1, m.reshape(bsz, seq, D), lat(0, 5), row(norm_g[0, 3]), tm=tm)

    u = _prenorm_matmul(x, row(norm_g[1, 0]), lat(1, 0), lat(1, 1), w_pw1[0].astype(bf16), row(b_pw1[0]),
                        mode="glu", tm=tm, out_dtype=f32)
    epi = (x, lat(1, 2), row(norm_g[1, 1]), row(norm_g[1, 2]), lat(1, 3), lat(1, 4),
           w_router[1].astype(bf16), row(b_router[1]))
    x1, h2, ti, tg = _conv_tail(u, w_dw[0], row(b_dw[0]), row(cn_g[0]), row(cn_b[0]),
                                w_pw2[0].astype(bf16), row(b_pw2[0]), epi, tm=min(256, seq))
    m = _moe(h2.reshape(-1, D), ti.reshape(-1, TOP_K), tg.reshape(-1, TOP_K),
             w_gu[1].astype(bf16), b_gu[1], w_down[1].astype(bf16), b_down[1])
    return _moe_out(x1, m.reshape(bsz, seq, D), lat(1, 5), row(norm_g[1, 3]), tm=tm)
```

```python
import functools

import jax
import jax.numpy as jnp
from jax import lax
from jax.experimental import pallas as pl
from jax.experimental.pallas import tpu as pltpu

D = 1024
GRID_W = 64
N_HEADS = 16
HEAD_DIM = 64
HEAD_PAIRS = N_HEADS // 2
WIN_ROWS = 8
WIN_COLS = 16
CONV_W = 31
CONV_HALO = 16
N_EXPERTS = 32
TOP_K = 4
SWIGLU_LIMIT = 7.0
SWIGLU_ALPHA = 1.702
EXPERT_BLOCK = 256
EPS = 1e-6
NEG = -1e30
NA_CHUNK_ROWS = 8
LANE = 128
SUB = 8
MOE_TILE = 1024
MOE_TILE_BITS = MOE_TILE.bit_length() - 1
PAD_BITS = EXPERT_BLOCK.bit_length() - 2
V7X_VMEM_LIMIT = 56 * 1024 * 1024

f32 = jnp.float32
bf16 = jnp.bfloat16


def _params(*sem):
    return pltpu.CompilerParams(dimension_semantics=sem, vmem_limit_bytes=V7X_VMEM_LIMIT)


def _rms(v, g):
    return v * lax.rsqrt(jnp.mean(v * v, axis=-1, keepdims=True) + EPS) * g


def _ada_kernel(c_ref, w_ref, b_ref, o_ref):
    c = c_ref[...]
    s = c * jax.nn.sigmoid(c)
    o_ref[...] = jnp.dot(s, w_ref[...], preferred_element_type=f32,
                         precision=lax.Precision.HIGHEST) + b_ref[...]


def _ada(cvec, w_ada, b_ada):
    depth, _, n = w_ada.shape
    tn = 1536
    return pl.pallas_call(
        _ada_kernel,
        grid=(depth, n // tn),
        in_specs=[pl.BlockSpec((8, D), lambda i, j: (0, 0)),
                  pl.BlockSpec((None, D, tn), lambda i, j: (i, 0, j)),
                  pl.BlockSpec((None, 1, tn), lambda i, j: (i, 0, j))],
        out_specs=pl.BlockSpec((None, 8, tn), lambda i, j: (i, 0, j)),
        out_shape=jax.ShapeDtypeStruct((depth, 8, n), f32),
        compiler_params=_params("arbitrary", "arbitrary"),
        name="ada",
    )(cvec, w_ada, b_ada.reshape(depth, 1, n))


def _prenorm_matmul_kernel(x_ref, g_ref, sh_ref, sc_ref, w_ref, b_ref, o_ref, *, mode, tn):
    h = _rms(x_ref[...], g_ref[...]) * (1.0 + sc_ref[...]) + sh_ref[...]
    hb = h.astype(bf16)
    n_out = o_ref.shape[-1]
    for j in range(n_out // tn):
        if mode == "glu":
            a = jnp.dot(hb, w_ref[:, j * tn:(j + 1) * tn], preferred_element_type=f32) + b_ref[:, j * tn:(j + 1) * tn]
            g = (jnp.dot(hb, w_ref[:, n_out + j * tn:n_out + (j + 1) * tn], preferred_element_type=f32)
                 + b_ref[:, n_out + j * tn:n_out + (j + 1) * tn])
            o_ref[:, j * tn:(j + 1) * tn] = (a * jax.nn.sigmoid(g)).astype(o_ref.dtype)
        else:
            a = jnp.dot(hb, w_ref[:, j * tn:(j + 1) * tn], preferred_element_type=f32)
            if mode == "qkv" and (j + 1) * tn <= D:
                a = a * (HEAD_DIM ** -0.5)
            o_ref[:, j * tn:(j + 1) * tn] = a.astype(o_ref.dtype)


def _prenorm_matmul(x, g, sh, sc, w, b, *, mode, tm, out_dtype):
    bsz, seq, _ = x.shape
    n = w.shape[1]
    n_out = n // 2 if mode == "glu" else n
    tn = 512
    kern = functools.partial(_prenorm_matmul_kernel, mode=mode, tn=tn)
    return pl.pallas_call(
        kern,
        grid=(bsz, seq // tm),
        in_specs=[pl.BlockSpec((None, tm, D), lambda b_, t: (b_, t, 0)),
                  pl.BlockSpec((1, D), lambda b_, t: (0, 0)),
                  pl.BlockSpec((None, 1, D), lambda b_, t: (b_, 0, 0)),
                  pl.BlockSpec((None, 1, D), lambda b_, t: (b_, 0, 0)),
                  pl.BlockSpec((D, n), lambda b_, t: (0, 0)),
                  pl.BlockSpec((1, n), lambda b_, t: (0, 0))],
        out_specs=pl.BlockSpec((None, tm, n_out), lambda b_, t: (b_, t, 0)),
        out_shape=jax.ShapeDtypeStruct((bsz, seq, n_out), out_dtype),
        compiler_params=_params("parallel", "parallel"),
        name="prenorm_matmul_" + mode,
    )(x, g, sh, sc, w, b)


def _na_kernel(q_ref, kp_ref, kc_ref, kn_ref, vp_ref, vc_ref, vn_ref, kx_ref, vx_ref, bias_ref, o_ref,
               k_buf, v_buf, *, rows):
    c = pl.program_id(2)
    blk = NA_CHUNK_ROWS * GRID_W
    k_buf[0:blk] = kp_ref[...]
    k_buf[blk:2 * blk] = kc_ref[...]
    k_buf[2 * blk:3 * blk] = kn_ref[...]
    v_buf[0:blk] = vp_ref[...]
    v_buf[blk:2 * blk] = vc_ref[...]
    v_buf[2 * blk:3 * blk] = vn_ref[...]
    lane = lax.broadcasted_iota(jnp.int32, (GRID_W, 2 * HEAD_DIM), 1)
    first_head = lane < HEAD_DIM
    nt = (((1,), (1,)), ((), ()))
    tn = (((0,), (0,)), ((), ()))
    kx = kx_ref[...]
    vx = vx_ref[...]
    win = WIN_ROWS * GRID_W

    def body(i, carry):
        r = c * NA_CHUNK_ROWS + i
        rs = jnp.clip(r - WIN_ROWS // 2, 0, rows - WIN_ROWS)
        delta = r - rs
        start = pl.multiple_of((rs - (c - 1) * NA_CHUNK_ROWS) * GRID_W, GRID_W)
        q = q_ref[pl.ds(pl.multiple_of(i * GRID_W, GRID_W), GRID_W), :]
        zero = jnp.zeros_like(q)
        qbd = jnp.concatenate([jnp.where(first_head, q, zero), jnp.where(first_head, zero, q)], axis=0)
        kw = k_buf[pl.ds(start, win), :]
        s_lat = lax.dot_general(kw, qbd, nt, preferred_element_type=f32) + bias_ref[delta]
        s_ctx = lax.dot_general(kx, qbd, nt, preferred_element_type=f32)
        m = jnp.maximum(jnp.max(s_lat, axis=0, keepdims=True), jnp.max(s_ctx, axis=0, keepdims=True))
        p_lat = jnp.exp(s_lat - m)
        p_ctx = jnp.exp(s_ctx - m)
        inv = 1.0 / (jnp.sum(p_lat, axis=0, keepdims=True) + jnp.sum(p_ctx, axis=0, keepdims=True))
        p_lat = (p_lat * inv).astype(bf16)
        p_ctx = (p_ctx * inv).astype(bf16)
        vw = v_buf[pl.ds(start, win), :]
        o = (lax.dot_general(p_lat, vw, tn, preferred_element_type=f32)
             + lax.dot_general(p_ctx, vx, tn, preferred_element_type=f32))
        o_ref[pl.ds(pl.multiple_of(i * GRID_W, GRID_W), GRID_W), :] = jnp.where(
            first_head, o[0:GRID_W], o[GRID_W:2 * GRID_W]).astype(o_ref.dtype)
        return carry

    lax.fori_loop(0, NA_CHUNK_ROWS, body, 0)


def _na_bias_table(rpb):
    qc = jnp.arange(GRID_W)
    cs = jnp.clip(qc - WIN_COLS // 2, 0, GRID_W - WIN_COLS)
    kc = jnp.arange(GRID_W)
    inside = (kc[None, :] >= cs[:, None]) & (kc[None, :] < cs[:, None] + WIN_COLS)
    col_off = kc[None, :] - qc[:, None] + (WIN_COLS - 1)
    col_sel = ((col_off[:, :, None] == jnp.arange(2 * WIN_COLS - 1)) & inside[:, :, None]).astype(f32)
    t = jnp.einsum('hrc,qkc->hrqk', rpb.astype(f32), col_sel, precision=lax.Precision.HIGHEST)
    t = jnp.where(inside[None, None], t, NEG)
    t = jnp.stack([t[:, WIN_ROWS - 1 - d:2 * WIN_ROWS - 1 - d] for d in range(WIN_ROWS)], axis=1)
    t = jnp.transpose(t, (1, 2, 4, 0, 3))
    t = t.reshape(WIN_ROWS, WIN_ROWS * GRID_W, HEAD_PAIRS, 2 * GRID_W)
    return jnp.transpose(t, (2, 0, 1, 3))


def _na_attention(qkv, kv_ctx, bias_t):
    bsz, seq, _ = qkv.shape
    ctx_len = kv_ctx.shape[1]
    rows = seq // GRID_W
    blk = NA_CHUNK_ROWS * GRID_W
    n_chunks = seq // blk
    hp2 = 2 * HEAD_DIM
    kblk = D // hp2
    vblk = 2 * D // hp2

    def spec(col0, shift):
        return pl.BlockSpec((None, blk, hp2),
                            lambda b, h, c: (b, jnp.clip(c + shift, 0, n_chunks - 1), col0 + h))

    return pl.pallas_call(
        functools.partial(_na_kernel, rows=rows),
        grid=(bsz, HEAD_PAIRS, n_chunks),
        in_specs=[spec(0, 0),
                  spec(kblk, -1), spec(kblk, 0), spec(kblk, 1),
                  spec(vblk, -1), spec(vblk, 0), spec(vblk, 1),
                  pl.BlockSpec((None, ctx_len, hp2), lambda b, h, c: (b, 0, kblk + h)),
                  pl.BlockSpec((None, ctx_len, hp2), lambda b, h, c: (b, 0, vblk + h)),
                  pl.BlockSpec((None, WIN_ROWS, WIN_ROWS * GRID_W, hp2), lambda b, h, c: (h, 0, 0, 0))],
        out_specs=pl.BlockSpec((None, blk, hp2), lambda b, h, c: (b, c, h)),
        out_shape=jax.ShapeDtypeStruct((bsz, seq, D), bf16),
        scratch_shapes=[pltpu.VMEM((3 * blk, hp2), bf16), pltpu.VMEM((3 * blk, hp2), bf16)],
        compiler_params=_params("parallel", "parallel", "arbitrary"),
        name="na_attention",
    )(qkv, qkv, qkv, qkv, qkv, qkv, qkv, kv_ctx, kv_ctx, bias_t)


def _mixer_epilogue(y, x_ref, gate_ref, g_post_ref, g_pre_ref, sh_ref, sc_ref, wr_ref, br_ref,
                    x1_ref, h2_ref, ti_ref, tg_ref):
    x1 = x_ref[...] + gate_ref[...] * _rms(y, g_post_ref[...])
    x1_ref[...] = x1
    h2 = _rms(x1, g_pre_ref[...]) * (1.0 + sc_ref[...]) + sh_ref[...]
    h2_ref[...] = h2.reshape(h2_ref.shape)
    hb = h2.astype(bf16)
    logits = jnp.dot(hb, wr_ref[...], preferred_element_type=f32) + br_ref[...]
    eidx = lax.broadcasted_iota(jnp.int32, logits.shape, 1).astype(f32)
    work = logits
    vals, idxs = [], []
    for _ in range(TOP_K):
        m = jnp.max(work, axis=-1, keepdims=True)
        first = jnp.min(jnp.where(work == m, eidx, float(N_EXPERTS)), axis=-1, keepdims=True)
        vals.append(m)
        idxs.append(first)
        work = jnp.where(eidx == first, NEG, work)
    top_v = jnp.concatenate(vals, axis=-1)
    e = jnp.exp(top_v - vals[0])
    tg_ref[...] = e / jnp.sum(e, axis=-1, keepdims=True)
    ti_ref[...] = jnp.concatenate(idxs, axis=-1).astype(jnp.int32)


def _attn_out_kernel(a_ref, wo_ref, *rest):
    y = jnp.dot(a_ref[...], wo_ref[...], preferred_element_type=f32)
    _mixer_epilogue(y, *rest)


def _epilogue_specs(tm):
    tok = lambda b, t: (b, t, 0)
    per_b = lambda b, t: (b, 0, 0)
    const = lambda b, t: (0, 0)
    in_specs = [pl.BlockSpec((None, tm, D), tok),
                pl.BlockSpec((None, 1, D), per_b),
                pl.BlockSpec((1, D), const),
                pl.BlockSpec((1, D), const),
                pl.BlockSpec((None, 1, D), per_b),
                pl.BlockSpec((None, 1, D), per_b),
                pl.BlockSpec((D, N_EXPERTS), const),
                pl.BlockSpec((1, N_EXPERTS), const)]
    out_specs = [pl.BlockSpec((None, tm, D), tok),
                 pl.BlockSpec((None, tm, SUB, LANE), lambda b, t: (b, t, 0, 0)),
                 pl.BlockSpec((None, tm, TOP_K), tok),
                 pl.BlockSpec((None, tm, TOP_K), tok)]
    return in_specs, out_specs


def _epilogue_shapes(bsz, seq):
    return [jax.ShapeDtypeStruct((bsz, seq, D), f32),
            jax.ShapeDtypeStruct((bsz, seq, SUB, LANE), f32),
            jax.ShapeDtypeStruct((bsz, seq, TOP_K), jnp.int32),
            jax.ShapeDtypeStruct((bsz, seq, TOP_K), f32)]


def _attn_out(a, wo, epi_args, *, tm):
    bsz, seq, _ = a.shape
    in_specs, out_specs = _epilogue_specs(tm)
    return pl.pallas_call(
        _attn_out_kernel,
        grid=(bsz, seq // tm),
        in_specs=[pl.BlockSpec((None, tm, D), lambda b, t: (b, t, 0)),
                  pl.BlockSpec((D, D), lambda b, t: (0, 0))] + in_specs,
        out_specs=out_specs,
        out_shape=_epilogue_shapes(bsz, seq),
        compiler_params=_params("parallel", "parallel"),
        name="attn_out",
    )(a, wo, *epi_args)


def _conv_kernel(up_ref, uc_ref, un_ref, wdw_ref, bdw_ref, lg_ref, lb_ref, w2_ref, b2_ref, *rest, tm, sub):
    ext, = rest[-1:]
    t = pl.program_id(1)
    nt = pl.num_programs(1)
    ext[0:CONV_HALO] = jnp.where(t > 0, up_ref[...], 0.0)
    ext[CONV_HALO:CONV_HALO + tm] = uc_ref[...]
    ext[CONV_HALO + tm:] = jnp.where(t < nt - 1, un_ref[...], 0.0)
    base = CONV_HALO - CONV_W // 2
    parts = []
    for rc in range(tm // sub):
        acc = jnp.zeros((sub, D), f32) + bdw_ref[...]
        for k in range(CONV_W):
            acc = acc + ext[rc * sub + base + k:rc * sub + base + k + sub, :] * wdw_ref[k:k + 1, :]
        mu = jnp.mean(acc, axis=-1, keepdims=True)
        d = acc - mu
        var = jnp.mean(d * d, axis=-1, keepdims=True)
        z = d * lax.rsqrt(var + EPS) * lg_ref[...] + lb_ref[...]
        parts.append((z * jax.nn.sigmoid(z)).astype(bf16))
    z = jnp.concatenate(parts, axis=0)
    y = jnp.dot(z, w2_ref[...], preferred_element_type=f32) + b2_ref[...]
    _mixer_epilogue(y, *rest[:-1])


def _conv_tail(u, w_dw, b_dw, ln_g, ln_b, w2, b2, epi_args, *, tm):
    bsz, seq, _ = u.shape
    in_specs, out_specs = _epilogue_specs(tm)
    per = tm // CONV_HALO
    n_halo = seq // CONV_HALO
    const = lambda b, t: (0, 0)
    return pl.pallas_call(
        functools.partial(_conv_kernel, tm=tm, sub=32),
        grid=(bsz, seq // tm),
        in_specs=[pl.BlockSpec((None, CONV_HALO, D), lambda b, t: (b, jnp.maximum(t * per - 1, 0), 0)),
                  pl.BlockSpec((None, tm, D), lambda b, t: (b, t, 0)),
                  pl.BlockSpec((None, CONV_HALO, D), lambda b, t: (b, jnp.minimum((t + 1) * per, n_halo - 1), 0)),
                  pl.BlockSpec((CONV_W, D), const),
                  pl.BlockSpec((1, D), const),
                  pl.BlockSpec((1, D), const),
                  pl.BlockSpec((1, D), const),
                  pl.BlockSpec((D, D), const),
                  pl.BlockSpec((1, D), const)] + in_specs,
        out_specs=out_specs,
        out_shape=_epilogue_shapes(bsz, seq),
        scratch_shapes=[pltpu.VMEM((tm + 2 * CONV_HALO, D), f32)],
        compiler_params=_params("parallel", "arbitrary"),
        name="conv_tail",
    )(u, u, u, w_dw, b_dw, ln_g, ln_b, w2, b2, *epi_args)


def _row_copies(src, src_off, dst, dst_off, n, sem, max_bit, fixed_src=False):
    out = []
    for b in range(max_bit, -1, -1):
        size = 1 << b
        start = (n >> (b + 1)) << (b + 1)
        s_off = src_off if fixed_src else src_off + start
        cp = pltpu.make_async_copy(src.at[pl.ds(s_off, size)], dst.at[pl.ds(dst_off + start, size)], sem)
        out.append((((n >> b) & 1) == 1, cp))
    return out


def _start_all(copies):
    for cond, cp in copies:
        pl.when(cond)(cp.start)


def _wait_all(copies):
    for cond, cp in copies:
        pl.when(cond)(cp.wait)


def _dispatch_kernel(nseg_ref, pos_ref, off_ref, padn_ref, padpos_ref, slots_hbm, x_ref, xs_hbm,
                     slot_s, stage, zeros, sem_s, sem):
    i = pl.program_id(0)
    t4 = TOP_K * MOE_TILE
    cp = pltpu.make_async_copy(slots_hbm.at[pl.ds(i * t4, t4)], slot_s, sem_s)
    cp.start()
    cp.wait()

    def tok(tb, carry):
        for u in range(8):
            t = tb * 8 + u
            v = x_ref[t]
            for k in range(TOP_K):
                stage[slot_s[t * TOP_K + k]] = v
        return carry

    lax.fori_loop(0, MOE_TILE // 8, tok, 0)

    def seg(e):
        j = i * N_EXPERTS + e
        return _row_copies(stage, off_ref[j], xs_hbm, pos_ref[j], nseg_ref[j], sem, MOE_TILE_BITS)

    def seg_start(e, carry):
        _start_all(seg(e))
        return carry

    def seg_wait(e, carry):
        _wait_all(seg(e))
        return carry

    lax.fori_loop(0, N_EXPERTS, seg_start, 0)
    lax.fori_loop(0, N_EXPERTS, seg_wait, 0)

    @pl.when(i == pl.num_programs(0) - 1)
    def _():
        zeros[...] = jnp.zeros_like(zeros)

        def pad(e):
            return _row_copies(zeros, 0, xs_hbm, padpos_ref[e], padn_ref[e], sem, PAD_BITS, fixed_src=True)

        def pad_start(e, carry):
            _start_all(pad(e))
            return carry

        def pad_wait(e, carry):
            _wait_all(pad(e))
            return carry

        lax.fori_loop(0, N_EXPERTS, pad_start, 0)
        lax.fori_loop(0, N_EXPERTS, pad_wait, 0)

        tail0 = padpos_ref[N_EXPERTS]
        zrows = zeros.shape[0]

        def tail(j):
            return pltpu.make_async_copy(zeros, xs_hbm.at[pl.ds(tail0 + j * zrows, zrows)], sem)

        def tail_start(j, carry):
            tail(j).start()
            return carry

        def tail_wait(j, carry):
            tail(j).wait()
            return carry

        n_tail = (xs_hbm.shape[0] - tail0) // zrows
        lax.fori_loop(0, n_tail, tail_start, 0)
        lax.fori_loop(0, n_tail, tail_wait, 0)


def _dispatch(h2t, slots, nseg, pos, off, padn, padpos, n_rows):
    n_tok = h2t.shape[0]
    nt = n_tok // MOE_TILE
    grid_spec = pltpu.PrefetchScalarGridSpec(
        num_scalar_prefetch=5,
        grid=(nt,),
        in_specs=[pl.BlockSpec(memory_space=pl.ANY),
                  pl.BlockSpec((MOE_TILE, SUB, LANE), lambda i, *_: (i, 0, 0))],
        out_specs=pl.BlockSpec(memory_space=pl.ANY),
        scratch_shapes=[pltpu.SMEM((TOP_K * MOE_TILE,), jnp.int32),
                        pltpu.VMEM((TOP_K * MOE_TILE, SUB, LANE), f32),
                        pltpu.VMEM((1 << PAD_BITS, SUB, LANE), f32),
                        pltpu.SemaphoreType.DMA(()),
                        pltpu.SemaphoreType.DMA(())],
    )
    return pl.pallas_call(
        _dispatch_kernel,
        grid_spec=grid_spec,
        out_shape=jax.ShapeDtypeStruct((n_rows, SUB, LANE), f32),
        compiler_params=_params("arbitrary"),
        name="moe_dispatch",
    )(nseg, pos, off, padn, padpos, slots, h2t)


def _experts_kernel(be_ref, bv_ref, bf_ref, xs_ref, wgu_ref, bgu_ref, wd_ref, bd_ref, y_ref, wgu_s, wd_s):
    b = pl.program_id(0)

    @pl.when(bf_ref[b] == 1)
    def _():
        wgu_s[...] = wgu_ref[...].astype(bf16)
        wd_s[...] = wd_ref[...].astype(bf16)

    @pl.when(bv_ref[b] == 1)
    def _():
        x = xs_ref[...].reshape(EXPERT_BLOCK, D).astype(bf16)
        gu = jnp.dot(x, wgu_s[...], preferred_element_type=f32) + bgu_ref[...]
        gate = jnp.minimum(gu[:, :D], SWIGLU_LIMIT)
        up = jnp.clip(gu[:, D:], -SWIGLU_LIMIT, SWIGLU_LIMIT)
        glu = gate * jax.nn.sigmoid(gate * SWIGLU_ALPHA)
        a = ((up + 1.0) * glu).astype(bf16)
        y = jnp.dot(a, wd_s[...], preferred_element_type=f32) + bd_ref[...]
        y_ref[...] = y.reshape(EXPERT_BLOCK, SUB, LANE)

    @pl.when(bv_ref[b] == 0)
    def _():
        y_ref[...] = jnp.zeros_like(y_ref)


def _experts(xs, blk_e, blk_valid, blk_first, w_gu, b_gu, w_down, b_down):
    n_rows = xs.shape[0]
    n_blocks = n_rows // EXPERT_BLOCK
    grid_spec = pltpu.PrefetchScalarGridSpec(
        num_scalar_prefetch=3,
        grid=(n_blocks,),
        in_specs=[pl.BlockSpec((EXPERT_BLOCK, SUB, LANE), lambda b, be, bv, bf: (b, 0, 0)),
                  pl.BlockSpec((None, D, 2 * D), lambda b, be, bv, bf: (be[b], 0, 0)),
                  pl.BlockSpec((None, 1, 2 * D), lambda b, be, bv, bf: (be[b], 0, 0)),
                  pl.BlockSpec((None, D, D), lambda b, be, bv, bf: (be[b], 0, 0)),
                  pl.BlockSpec((None, 1, D), lambda b, be, bv, bf: (be[b], 0, 0))],
        out_specs=pl.BlockSpec((EXPERT_BLOCK, SUB, LANE), lambda b, be, bv, bf: (b, 0, 0)),
        scratch_shapes=[pltpu.VMEM((D, 2 * D), bf16), pltpu.VMEM((D, D), bf16)],
    )
    return pl.pallas_call(
        _experts_kernel,
        grid_spec=grid_spec,
        out_shape=jax.ShapeDtypeStruct((n_rows, SUB, LANE), f32),
        compiler_params=_params("arbitrary"),
        name="experts",
    )(blk_e, blk_valid, blk_first, xs, w_gu, b_gu.reshape(N_EXPERTS, 1, 2 * D), w_down,
      b_down.reshape(N_EXPERTS, 1, D))


def _combine_kernel(nseg_ref, pos_ref, off_ref, slots_hbm, gates_hbm, ys_hbm, x1_ref, gate_ref, g_ref, o_ref,
                    slot_s, gate_s, ybuf, mbuf, sem_s, sem):
    i = pl.program_id(0)
    t4 = TOP_K * MOE_TILE
    cps = pltpu.make_async_copy(slots_hbm.at[pl.ds(i * t4, t4)], slot_s, sem_s.at[0])
    cpg = pltpu.make_async_copy(gates_hbm.at[pl.ds(i * t4, t4)], gate_s, sem_s.at[1])
    cps.start()
    cpg.start()

    def seg(e):
        j = i * N_EXPERTS + e
        return _row_copies(ys_hbm, pos_ref[j], ybuf, off_ref[j], nseg_ref[j], sem, MOE_TILE_BITS)

    def seg_start(e, carry):
        _start_all(seg(e))
        return carry

    def seg_wait(e, carry):
        _wait_all(seg(e))
        return carry

    lax.fori_loop(0, N_EXPERTS, seg_start, 0)
    cps.wait()
    cpg.wait()
    lax.fori_loop(0, N_EXPERTS, seg_wait, 0)

    def tok(tb, carry):
        for u in range(8):
            t = tb * 8 + u
            acc = gate_s[t * TOP_K] * ybuf[slot_s[t * TOP_K]]
            for k in range(1, TOP_K):
                acc = acc + gate_s[t * TOP_K + k] * ybuf[slot_s[t * TOP_K + k]]
            mbuf[t] = acc
        return carry

    lax.fori_loop(0, MOE_TILE // 8, tok, 0)
    for r0 in range(0, MOE_TILE, EXPERT_BLOCK):
        m = mbuf[r0:r0 + EXPERT_BLOCK].reshape(EXPERT_BLOCK, D)
        o_ref[r0:r0 + EXPERT_BLOCK, :] = (x1_ref[r0:r0 + EXPERT_BLOCK, :]
                                          + gate_ref[...] * _rms(m, g_ref[...]))


def _combine(ys, slots, gates, nseg, pos, off, x1, gate, g, tiles_per_batch):
    n_tok = x1.shape[0]
    nt = n_tok // MOE_TILE
    grid_spec = pltpu.PrefetchScalarGridSpec(
        num_scalar_prefetch=3,
        grid=(nt,),
        in_specs=[pl.BlockSpec(memory_space=pl.ANY),
                  pl.BlockSpec(memory_space=pl.ANY),
                  pl.BlockSpec(memory_space=pl.ANY),
                  pl.BlockSpec((MOE_TILE, D), lambda i, *_: (i, 0)),
                  pl.BlockSpec((None, 1, D), lambda i, *_: (i // tiles_per_batch, 0, 0)),
                  pl.BlockSpec((1, D), lambda i, *_: (0, 0))],
        out_specs=pl.BlockSpec((MOE_TILE, D), lambda i, *_: (i, 0)),
        scratch_shapes=[pltpu.SMEM((TOP_K * MOE_TILE,), jnp.int32),
                        pltpu.SMEM((TOP_K * MOE_TILE,), f32),
                        pltpu.VMEM((TOP_K * MOE_TILE, SUB, LANE), f32),
                        pltpu.VMEM((MOE_TILE, SUB, LANE), f32),
                        pltpu.SemaphoreType.DMA((2,)),
                        pltpu.SemaphoreType.DMA(())],
    )
    return pl.pallas_call(
        _combine_kernel,
        grid_spec=grid_spec,
        out_shape=jax.ShapeDtypeStruct((n_tok, D), f32),
        compiler_params=_params("arbitrary"),
        name="moe_combine",
    )(nseg, pos, off, slots, gates, ys, x1, gate, g)


def _moe_tables(top_i):
    n_tok = top_i.shape[0]
    nt = n_tok // MOE_TILE
    oh4 = (top_i[:, :, None] == jnp.arange(N_EXPERTS)[None, None, :]).astype(f32)
    oh = oh4.sum(axis=1).reshape(nt, MOE_TILE, N_EXPERTS)
    tri = (jnp.arange(MOE_TILE)[:, None] > jnp.arange(MOE_TILE)[None, :]).astype(bf16)
    rank = jnp.einsum('ab,nbe->nae', tri, oh.astype(bf16), preferred_element_type=f32)
    nseg = oh.sum(axis=1).astype(jnp.int32)
    counts = nseg.sum(axis=0)
    pcounts = (counts + EXPERT_BLOCK - 1) // EXPERT_BLOCK * EXPERT_BLOCK
    pend = jnp.cumsum(pcounts)
    pstart = pend - pcounts
    pos = pstart[None, :] + jnp.cumsum(nseg, axis=0) - nseg
    off = jnp.cumsum(nseg, axis=1) - nseg
    base = (off.astype(f32)[:, None, :] + rank).reshape(n_tok, 1, N_EXPERTS)
    slots = jnp.sum(oh4 * base, axis=-1).astype(jnp.int32).reshape(-1)
    n_blocks = -(-(n_tok * TOP_K) // EXPERT_BLOCK) + N_EXPERTS
    blk_start = jnp.arange(n_blocks, dtype=jnp.int32) * EXPERT_BLOCK
    blk_e = jnp.minimum(jnp.searchsorted(pend, blk_start, side='right'), N_EXPERTS - 1).astype(jnp.int32)
    blk_valid = (blk_start < pend[-1]).astype(jnp.int32)
    blk_first = jnp.concatenate([jnp.ones((1,), jnp.int32), (blk_e[1:] != blk_e[:-1]).astype(jnp.int32)])
    tabs = dict(nseg=nseg.reshape(-1), pos=pos.reshape(-1).astype(jnp.int32), off=off.reshape(-1),
                padn=(pcounts - counts).astype(jnp.int32),
                padpos=jnp.concatenate([pstart + counts, pend[-1:]]).astype(jnp.int32),
                slots=slots, blk_e=blk_e, blk_valid=blk_valid, blk_first=blk_first)
    return tabs, n_blocks * EXPERT_BLOCK


def _moe_residual(h2t, top_i, top_g, x1, gate, g, w_gu, b_gu, w_down, b_down, tiles_per_batch):
    t, n_rows = _moe_tables(top_i)
    xs = _dispatch(h2t, t["slots"], t["nseg"], t["pos"], t["off"], t["padn"], t["padpos"], n_rows)
    ys = _experts(xs, t["blk_e"], t["blk_valid"], t["blk_first"], w_gu, b_gu, w_down, b_down)
    return _combine(ys, t["slots"], top_g.reshape(-1), t["nseg"], t["pos"], t["off"], x1, gate, g,
                    tiles_per_batch)


def kernel(x, c, ctx, c_ctx, w_ada, b_ada, norm_g, w_qkv, w_o, rpb, w_pw1, b_pw1, w_dw, b_dw, cn_g, cn_b,
           w_pw2, b_pw2, w_router, b_router, w_gu, b_gu, w_down, b_down):
    bsz, seq, _ = x.shape
    tm = min(512, seq)
    cvec = jnp.zeros((8, D), f32).at[:bsz].set(c).at[bsz].set(c_ctx)
    mod = _ada(cvec, w_ada, b_ada)

    def lat(i, k):
        return mod[i, :bsz, k * D:(k + 1) * D][:, None, :]

    def cx(i, k):
        return jnp.broadcast_to(mod[i, bsz, k * D:(k + 1) * D][None, None, :], (bsz, 1, D))

    row = lambda v: v.reshape(1, -1)
    zero_b = jnp.zeros((1, 3 * D), f32)

    wq = w_qkv[0].astype(bf16)
    qkv = _prenorm_matmul(x, row(norm_g[0, 0]), lat(0, 0), lat(0, 1), wq, zero_b,
                          mode="qkv", tm=tm, out_dtype=bf16)
    kv_ctx = _prenorm_matmul(ctx, row(norm_g[0, 0]), cx(0, 0), cx(0, 1), wq, zero_b,
                             mode="qkv", tm=ctx.shape[1], out_dtype=bf16)
    attn = _na_attention(qkv, kv_ctx, _na_bias_table(rpb[0]))
    epi = (x, lat(0, 2), row(norm_g[0, 1]), row(norm_g[0, 2]), lat(0, 3), lat(0, 4),
           w_router[0].astype(bf16), row(b_router[0]))
    x1, h2, ti, tg = _attn_out(attn, w_o[0].astype(bf16), epi, tm=tm)
    x = _moe_residual(h2.reshape(-1, SUB, LANE), ti.reshape(-1, TOP_K), tg.reshape(-1, TOP_K),
                      x1.reshape(-1, D), lat(0, 5), row(norm_g[0, 3]),
                      w_gu[0], b_gu[0], w_down[0], b_down[0], seq // MOE_TILE).reshape(bsz, seq, D)

    u = _prenorm_matmul(x, row(norm_g[1, 0]), lat(1, 0), lat(1, 1), w_pw1[0].astype(bf16), row(b_pw1[0]),
                        mode="glu", tm=tm, out_dtype=f32)
    epi = (x, lat(1, 2), row(norm_g[1, 1]), row(norm_g[1, 2]), lat(1, 3), lat(1, 4),
           w_router[1].astype(bf16), row(b_router[1]))
    x1, h2, ti, tg = _conv_tail(u, w_dw[0], row(b_dw[0]), row(cn_g[0]), row(cn_b[0]),
                                w_pw2[0].astype(bf16), row(b_pw2[0]), epi, tm=min(256, seq))
    return _moe_residual(h2.reshape(-1, SUB, LANE), ti.reshape(-1, TOP_K), tg.reshape(-1, TOP_K),
                         x1.reshape(-1, D), lat(1, 5), row(norm_g[1, 3]),
                         w_gu[1], b_gu[1], w_down[1], b_down[1], seq // MOE_TILE).reshape(bsz, seq, D)
```

```python
import functools

import jax
import jax.numpy as jnp
from jax import lax
from jax.experimental import pallas as pl
from jax.experimental.pallas import tpu as pltpu

D = 1024
GRID_W = 64
N_HEADS = 16
HEAD_DIM = 64
HEAD_PAIRS = N_HEADS // 2
WIN_ROWS = 8
WIN_COLS = 16
CONV_W = 31
CONV_HALO = 16
N_EXPERTS = 32
TOP_K = 4
SWIGLU_LIMIT = 7.0
SWIGLU_ALPHA = 1.702
EXPERT_BLOCK = 256
EXPERT_CHUNK = 128
EPS = 1e-6
NEG = -1e30
NA_CHUNK_ROWS = 8
LOG2E = 1.4426950408889634
Q_SCALE = HEAD_DIM ** -0.5 * LOG2E
LANE = 128
SUB = 8
MOE_TILE = 512
MOE_TILE_BITS = MOE_TILE.bit_length() - 1
PAD_BITS = EXPERT_BLOCK.bit_length() - 2
V7X_VMEM_LIMIT = 56 * 1024 * 1024

f32 = jnp.float32
bf16 = jnp.bfloat16


def _params(*sem):
    return pltpu.CompilerParams(dimension_semantics=sem, vmem_limit_bytes=V7X_VMEM_LIMIT)


def _rms(v, g):
    return v * lax.rsqrt(jnp.mean(v * v, axis=-1, keepdims=True) + EPS) * g


def _ada_kernel(c_ref, w_ref, b_ref, o_ref):
    c = c_ref[...]
    s = c * jax.nn.sigmoid(c)
    o_ref[...] = jnp.dot(s, w_ref[...], preferred_element_type=f32,
                         precision=lax.Precision.HIGHEST) + b_ref[...]


def _ada(cvec, w_ada, b_ada):
    depth, _, n = w_ada.shape
    tn = 1536
    return pl.pallas_call(
        _ada_kernel,
        grid=(depth, n // tn),
        in_specs=[pl.BlockSpec((8, D), lambda i, j: (0, 0)),
                  pl.BlockSpec((None, D, tn), lambda i, j: (i, 0, j)),
                  pl.BlockSpec((None, 1, tn), lambda i, j: (i, 0, j))],
        out_specs=pl.BlockSpec((None, 8, tn), lambda i, j: (i, 0, j)),
        out_shape=jax.ShapeDtypeStruct((depth, 8, n), f32),
        compiler_params=_params("arbitrary", "arbitrary"),
        name="ada",
    )(cvec, w_ada, b_ada.reshape(depth, 1, n))


def _prenorm_matmul_kernel(x_ref, g_ref, sh_ref, sc_ref, w_ref, b_ref, o_ref, *, mode, tn):
    h = _rms(x_ref[...], g_ref[...]) * (1.0 + sc_ref[...]) + sh_ref[...]
    hb = h.astype(bf16)
    n_out = o_ref.shape[-1]
    for j in range(n_out // tn):
        if mode == "glu":
            a = jnp.dot(hb, w_ref[:, j * tn:(j + 1) * tn], preferred_element_type=f32) + b_ref[:, j * tn:(j + 1) * tn]
            g = (jnp.dot(hb, w_ref[:, n_out + j * tn:n_out + (j + 1) * tn], preferred_element_type=f32)
                 + b_ref[:, n_out + j * tn:n_out + (j + 1) * tn])
            o_ref[:, j * tn:(j + 1) * tn] = (a * jax.nn.sigmoid(g)).astype(o_ref.dtype)
        else:
            a = jnp.dot(hb, w_ref[:, j * tn:(j + 1) * tn], preferred_element_type=f32)
            if mode == "qkv" and (j + 1) * tn <= D:
                a = a * Q_SCALE
            o_ref[:, j * tn:(j + 1) * tn] = a.astype(o_ref.dtype)


def _prenorm_matmul(x, g, sh, sc, w, b, *, mode, tm, out_dtype):
    bsz, seq, _ = x.shape
    n = w.shape[1]
    n_out = n // 2 if mode == "glu" else n
    tn = 512
    kern = functools.partial(_prenorm_matmul_kernel, mode=mode, tn=tn)
    return pl.pallas_call(
        kern,
        grid=(bsz, seq // tm),
        in_specs=[pl.BlockSpec((None, tm, D), lambda b_, t: (b_, t, 0)),
                  pl.BlockSpec((1, D), lambda b_, t: (0, 0)),
                  pl.BlockSpec((None, 1, D), lambda b_, t: (b_, 0, 0)),
                  pl.BlockSpec((None, 1, D), lambda b_, t: (b_, 0, 0)),
                  pl.BlockSpec((D, n), lambda b_, t: (0, 0)),
                  pl.BlockSpec((1, n), lambda b_, t: (0, 0))],
        out_specs=pl.BlockSpec((None, tm, n_out), lambda b_, t: (b_, t, 0)),
        out_shape=jax.ShapeDtypeStruct((bsz, seq, n_out), out_dtype),
        compiler_params=_params("parallel", "parallel"),
        name="prenorm_matmul_" + mode,
    )(x, g, sh, sc, w, b)


def _na_kernel(q_ref, kp_ref, kc_ref, kn_ref, vp_ref, vc_ref, vn_ref, kx_ref, vx_ref, bias_ref, o_ref,
               k_buf, v_buf, vx_buf, *, rows):
    c = pl.program_id(2)
    blk = NA_CHUNK_ROWS * GRID_W
    hp2 = 2 * HEAD_DIM
    k_buf[0:blk] = kp_ref[...]
    k_buf[blk:2 * blk] = kc_ref[...]
    k_buf[2 * blk:3 * blk] = kn_ref[...]
    v_buf[0:blk, 0:hp2] = vp_ref[...]
    v_buf[blk:2 * blk, 0:hp2] = vc_ref[...]
    v_buf[2 * blk:3 * blk, 0:hp2] = vn_ref[...]
    v_buf[:, hp2:] = jnp.ones((3 * blk, hp2), bf16)
    vx_buf[:, 0:hp2] = vx_ref[...]
    vx_buf[:, hp2:] = jnp.ones((vx_buf.shape[0], hp2), bf16)
    lane = lax.broadcasted_iota(jnp.int32, (GRID_W, hp2), 1)
    first_head = lane < HEAD_DIM
    nt = (((1,), (1,)), ((), ()))
    tn = (((0,), (0,)), ((), ()))
    win = WIN_ROWS * GRID_W

    for i in range(NA_CHUNK_ROWS):
        r = c * NA_CHUNK_ROWS + i
        rs = jnp.clip(r - WIN_ROWS // 2, 0, rows - WIN_ROWS)
        delta = r - rs
        start = pl.multiple_of((rs - (c - 1) * NA_CHUNK_ROWS) * GRID_W, GRID_W)
        q = q_ref[i * GRID_W:(i + 1) * GRID_W, :]
        zero = jnp.zeros_like(q)
        qbd = jnp.concatenate([jnp.where(first_head, q, zero), jnp.where(first_head, zero, q)], axis=0)
        s_lat = lax.dot_general(k_buf[pl.ds(start, win), :], qbd, nt, preferred_element_type=f32) + bias_ref[delta]
        s_ctx = lax.dot_general(kx_ref[...], qbd, nt, preferred_element_type=f32)
        m = jnp.maximum(jnp.max(s_lat, axis=0, keepdims=True), jnp.max(s_ctx, axis=0, keepdims=True))
        p_lat = jnp.exp2(s_lat - m).astype(bf16)
        p_ctx = jnp.exp2(s_ctx - m).astype(bf16)
        o = (lax.dot_general(p_lat, v_buf[pl.ds(start, win), :], tn, preferred_element_type=f32)
             + lax.dot_general(p_ctx, vx_buf[...], tn, preferred_element_type=f32))
        o = o[:, 0:hp2] * (1.0 / o[:, hp2:hp2 + 1])
        o_ref[i * GRID_W:(i + 1) * GRID_W, :] = jnp.where(
            first_head, o[0:GRID_W], o[GRID_W:2 * GRID_W]).astype(o_ref.dtype)


def _na_bias_table(rpb):
    qc = jnp.arange(GRID_W)
    cs = jnp.clip(qc - WIN_COLS // 2, 0, GRID_W - WIN_COLS)
    kc = jnp.arange(GRID_W)
    inside = (kc[None, :] >= cs[:, None]) & (kc[None, :] < cs[:, None] + WIN_COLS)
    col_off = kc[None, :] - qc[:, None] + (WIN_COLS - 1)
    col_sel = ((col_off[:, :, None] == jnp.arange(2 * WIN_COLS - 1)) & inside[:, :, None]).astype(f32)
    t = jnp.einsum('hrc,qkc->hrqk', rpb.astype(f32), col_sel, precision=lax.Precision.HIGHEST)
    t = jnp.where(inside[None, None], t * LOG2E, NEG)
    t = jnp.stack([t[:, WIN_ROWS - 1 - d:2 * WIN_ROWS - 1 - d] for d in range(WIN_ROWS)], axis=1)
    t = jnp.transpose(t, (1, 2, 4, 0, 3))
    t = t.reshape(WIN_ROWS, WIN_ROWS * GRID_W, HEAD_PAIRS, 2 * GRID_W)
    return jnp.transpose(t, (2, 0, 1, 3))


def _na_attention(qkv, kv_ctx, bias_t):
    bsz, seq, _ = qkv.shape
    ctx_len = kv_ctx.shape[1]
    rows = seq // GRID_W
    blk = NA_CHUNK_ROWS * GRID_W
    n_chunks = seq // blk
    hp2 = 2 * HEAD_DIM
    kblk = D // hp2
    vblk = 2 * D // hp2

    def spec(col0, shift):
        return pl.BlockSpec((None, blk, hp2),
                            lambda b, h, c: (b, jnp.clip(c + shift, 0, n_chunks - 1), col0 + h))

    return pl.pallas_call(
        functools.partial(_na_kernel, rows=rows),
        grid=(bsz, HEAD_PAIRS, n_chunks),
        in_specs=[spec(0, 0),
                  spec(kblk, -1), spec(kblk, 0), spec(kblk, 1),
                  spec(vblk, -1), spec(vblk, 0), spec(vblk, 1),
                  pl.BlockSpec((None, ctx_len, hp2), lambda b, h, c: (b, 0, kblk + h)),
                  pl.BlockSpec((None, ctx_len, hp2), lambda b, h, c: (b, 0, vblk + h)),
                  pl.BlockSpec((None, WIN_ROWS, WIN_ROWS * GRID_W, hp2), lambda b, h, c: (h, 0, 0, 0))],
        out_specs=pl.BlockSpec((None, blk, hp2), lambda b, h, c: (b, c, h)),
        out_shape=jax.ShapeDtypeStruct((bsz, seq, D), bf16),
        scratch_shapes=[pltpu.VMEM((3 * blk, hp2), bf16), pltpu.VMEM((3 * blk, 2 * hp2), bf16),
                        pltpu.VMEM((ctx_len, 2 * hp2), bf16)],
        compiler_params=_params("parallel", "parallel", "arbitrary"),
        name="na_attention",
    )(qkv, qkv, qkv, qkv, qkv, qkv, qkv, kv_ctx, kv_ctx, bias_t)


def _mixer_epilogue(y, x_ref, gate_ref, g_post_ref, g_pre_ref, sh_ref, sc_ref, wr_ref, br_ref,
                    x1_ref, h2_ref, ti_ref, tg_ref):
    x1 = x_ref[...] + gate_ref[...] * _rms(y, g_post_ref[...])
    x1_ref[...] = x1
    h2 = _rms(x1, g_pre_ref[...]) * (1.0 + sc_ref[...]) + sh_ref[...]
    h2_ref[...] = h2.reshape(h2_ref.shape)
    hb = h2.astype(bf16)
    logits = jnp.dot(hb, wr_ref[...], preferred_element_type=f32) + br_ref[...]
    eidx = lax.broadcasted_iota(jnp.int32, logits.shape, 1).astype(f32)
    work = logits
    vals, idxs = [], []
    for _ in range(TOP_K):
        m = jnp.max(work, axis=-1, keepdims=True)
        first = jnp.min(jnp.where(work == m, eidx, float(N_EXPERTS)), axis=-1, keepdims=True)
        vals.append(m)
        idxs.append(first)
        work = jnp.where(eidx == first, NEG, work)
    top_v = jnp.concatenate(vals, axis=-1)
    e = jnp.exp(top_v - vals[0])
    tg_ref[...] = e / jnp.sum(e, axis=-1, keepdims=True)
    ti_ref[...] = jnp.concatenate(idxs, axis=-1).astype(jnp.int32)


def _attn_out_kernel(a_ref, wo_ref, *rest):
    y = jnp.dot(a_ref[...], wo_ref[...], preferred_element_type=f32)
    _mixer_epilogue(y, *rest)


def _epilogue_specs(tm):
    tok = lambda b, t: (b, t, 0)
    per_b = lambda b, t: (b, 0, 0)
    const = lambda b, t: (0, 0)
    in_specs = [pl.BlockSpec((None, tm, D), tok),
                pl.BlockSpec((None, 1, D), per_b),
                pl.BlockSpec((1, D), const),
                pl.BlockSpec((1, D), const),
                pl.BlockSpec((None, 1, D), per_b),
                pl.BlockSpec((None, 1, D), per_b),
                pl.BlockSpec((D, N_EXPERTS), const),
                pl.BlockSpec((1, N_EXPERTS), const)]
    out_specs = [pl.BlockSpec((None, tm, D), tok),
                 pl.BlockSpec((None, tm, SUB, LANE), lambda b, t: (b, t, 0, 0)),
                 pl.BlockSpec((None, tm, TOP_K), tok),
                 pl.BlockSpec((None, tm, TOP_K), tok)]
    return in_specs, out_specs


def _epilogue_shapes(bsz, seq):
    return [jax.ShapeDtypeStruct((bsz, seq, D), f32),
            jax.ShapeDtypeStruct((bsz, seq, SUB, LANE), f32),
            jax.ShapeDtypeStruct((bsz, seq, TOP_K), jnp.int32),
            jax.ShapeDtypeStruct((bsz, seq, TOP_K), f32)]


def _attn_out(a, wo, epi_args, *, tm):
    bsz, seq, _ = a.shape
    in_specs, out_specs = _epilogue_specs(tm)
    return pl.pallas_call(
        _attn_out_kernel,
        grid=(bsz, seq // tm),
        in_specs=[pl.BlockSpec((None, tm, D), lambda b, t: (b, t, 0)),
                  pl.BlockSpec((D, D), lambda b, t: (0, 0))] + in_specs,
        out_specs=out_specs,
        out_shape=_epilogue_shapes(bsz, seq),
        compiler_params=_params("parallel", "parallel"),
        name="attn_out",
    )(a, wo, *epi_args)


def _conv_kernel(up_ref, uc_ref, un_ref, wdw_ref, bdw_ref, lg_ref, lb_ref, w2_ref, b2_ref, *rest, tm, sub):
    ext, shifted = rest[-2:]
    t = pl.program_id(1)
    nt = pl.num_programs(1)
    ext[0:CONV_HALO] = jnp.where(t > 0, up_ref[...], 0.0)
    ext[CONV_HALO:CONV_HALO + tm] = uc_ref[...]
    ext[CONV_HALO + tm:] = jnp.where(t < nt - 1, un_ref[...], 0.0)
    span = tm + 2 * CONV_HALO - SUB
    for a in range(1, SUB):
        shifted[a - 1, 0:span] = ext[a:a + span]
    base = CONV_HALO - CONV_W // 2
    parts = []
    for rc in range(tm // sub):
        acc = jnp.zeros((sub, D), f32) + bdw_ref[...]
        for k in range(CONV_W):
            a, r0 = (base + k) % SUB, rc * sub + (base + k) // SUB * SUB
            win = ext[r0:r0 + sub, :] if a == 0 else shifted[a - 1, r0:r0 + sub, :]
            acc = acc + win * wdw_ref[k:k + 1, :]
        mu = jnp.mean(acc, axis=-1, keepdims=True)
        d = acc - mu
        var = jnp.mean(d * d, axis=-1, keepdims=True)
        z = d * lax.rsqrt(var + EPS) * lg_ref[...] + lb_ref[...]
        parts.append((z * jax.nn.sigmoid(z)).astype(bf16))
    z = jnp.concatenate(parts, axis=0)
    y = jnp.dot(z, w2_ref[...], preferred_element_type=f32) + b2_ref[...]
    _mixer_epilogue(y, *rest[:-2])


def _conv_tail(u, w_dw, b_dw, ln_g, ln_b, w2, b2, epi_args, *, tm):
    bsz, seq, _ = u.shape
    in_specs, out_specs = _epilogue_specs(tm)
    per = tm // CONV_HALO
    n_halo = seq // CONV_HALO
    const = lambda b, t: (0, 0)
    return pl.pallas_call(
        functools.partial(_conv_kernel, tm=tm, sub=8),
        grid=(bsz, seq // tm),
        in_specs=[pl.BlockSpec((None, CONV_HALO, D), lambda b, t: (b, jnp.maximum(t * per - 1, 0), 0)),
                  pl.BlockSpec((None, tm, D), lambda b, t: (b, t, 0)),
                  pl.BlockSpec((None, CONV_HALO, D), lambda b, t: (b, jnp.minimum((t + 1) * per, n_halo - 1), 0)),
                  pl.BlockSpec((CONV_W, D), const),
                  pl.BlockSpec((1, D), const),
                  pl.BlockSpec((1, D), const),
                  pl.BlockSpec((1, D), const),
                  pl.BlockSpec((D, D), const),
                  pl.BlockSpec((1, D), const)] + in_specs,
        out_specs=out_specs,
        out_shape=_epilogue_shapes(bsz, seq),
        scratch_shapes=[pltpu.VMEM((tm + 2 * CONV_HALO, D), f32),
                        pltpu.VMEM((SUB - 1, tm + 2 * CONV_HALO, D), f32)],
        compiler_params=_params("parallel", "arbitrary"),
        name="conv_tail",
    )(u, u, u, w_dw, b_dw, ln_g, ln_b, w2, b2, *epi_args)


def _row_copies(src, src_off, dst, dst_off, n, sem, max_bit, fixed_src=False):
    out = []
    for b in range(max_bit, -1, -1):
        size = 1 << b
        start = (n >> (b + 1)) << (b + 1)
        s_off = src_off if fixed_src else src_off + start
        cp = pltpu.make_async_copy(src.at[pl.ds(s_off, size)], dst.at[pl.ds(dst_off + start, size)], sem)
        out.append((((n >> b) & 1) == 1, cp))
    return out


def _start_all(copies):
    for cond, cp in copies:
        pl.when(cond)(cp.start)


def _wait_all(copies):
    for cond, cp in copies:
        pl.when(cond)(cp.wait)


def _dispatch_kernel(nseg_ref, pos_ref, off_ref, padn_ref, padpos_ref, slots_hbm, x_ref, xs_hbm,
                     slot_s, stage, zeros, sem_s, sem):
    i = pl.program_id(0)
    last = pl.num_programs(0) - 1
    buf = i % 2
    t4 = TOP_K * MOE_TILE
    cp = pltpu.make_async_copy(slots_hbm.at[pl.ds(i * t4, t4)], slot_s, sem_s)
    cp.start()
    cp.wait()
    stage_b = stage.at[buf]

    def tok(tb, carry):
        for u in range(8):
            t = tb * 8 + u
            v = x_ref[t]
            for k in range(TOP_K):
                stage_b[slot_s[t * TOP_K + k]] = v
        return carry

    lax.fori_loop(0, MOE_TILE // 8, tok, 0)

    def seg(tile, b, e):
        j = tile * N_EXPERTS + e
        return _row_copies(stage.at[b], off_ref[j], xs_hbm, pos_ref[j], nseg_ref[j], sem.at[b], MOE_TILE_BITS)

    def for_experts(fn):
        def body(e, carry):
            fn(e)
            return carry
        lax.fori_loop(0, N_EXPERTS, body, 0)

    for_experts(lambda e: _start_all(seg(i, buf, e)))

    @pl.when(i > 0)
    def _():
        for_experts(lambda e: _wait_all(seg(i - 1, 1 - buf, e)))

    @pl.when(i == last)
    def _():
        for_experts(lambda e: _wait_all(seg(i, buf, e)))
        zeros[...] = jnp.zeros_like(zeros)
        zsem = sem.at[0]

        def pad(e):
            return _row_copies(zeros, 0, xs_hbm, padpos_ref[e], padn_ref[e], zsem, PAD_BITS, fixed_src=True)

        for_experts(lambda e: _start_all(pad(e)))
        for_experts(lambda e: _wait_all(pad(e)))

        tail0 = padpos_ref[N_EXPERTS]
        zrows = zeros.shape[0]

        def tail(j):
            return pltpu.make_async_copy(zeros, xs_hbm.at[pl.ds(tail0 + j * zrows, zrows)], zsem)

        def tail_start(j, carry):
            tail(j).start()
            return carry

        def tail_wait(j, carry):
            tail(j).wait()
            return carry

        n_tail = (xs_hbm.shape[0] - tail0) // zrows
        lax.fori_loop(0, n_tail, tail_start, 0)
        lax.fori_loop(0, n_tail, tail_wait, 0)


def _dispatch(h2t, slots, nseg, pos, off, padn, padpos, n_rows):
    n_tok = h2t.shape[0]
    nt = n_tok // MOE_TILE
    grid_spec = pltpu.PrefetchScalarGridSpec(
        num_scalar_prefetch=5,
        grid=(nt,),
        in_specs=[pl.BlockSpec(memory_space=pl.ANY),
                  pl.BlockSpec((MOE_TILE, SUB, LANE), lambda i, *_: (i, 0, 0))],
        out_specs=pl.BlockSpec(memory_space=pl.ANY),
        scratch_shapes=[pltpu.SMEM((TOP_K * MOE_TILE,), jnp.int32),
                        pltpu.VMEM((2, TOP_K * MOE_TILE, SUB, LANE), f32),
                        pltpu.VMEM((1 << PAD_BITS, SUB, LANE), f32),
                        pltpu.SemaphoreType.DMA(()),
                        pltpu.SemaphoreType.DMA((2,))],
    )
    return pl.pallas_call(
        _dispatch_kernel,
        grid_spec=grid_spec,
        out_shape=jax.ShapeDtypeStruct((n_rows, SUB, LANE), f32),
        compiler_params=_params("arbitrary"),
        name="moe_dispatch",
    )(nseg, pos, off, padn, padpos, slots, h2t)


def _experts_kernel(be_ref, bv_ref, bf_ref, xs_ref, wgu_ref, bgu_ref, wd_ref, bd_ref, y_ref, wgu_s, wd_s):
    b = pl.program_id(0)

    @pl.when(bf_ref[b] == 1)
    def _():
        wgu_s[...] = wgu_ref[...].astype(bf16)
        wd_s[...] = wd_ref[...].astype(bf16)

    @pl.when(bv_ref[b] == 1)
    def _():
        for r0 in range(0, EXPERT_BLOCK, EXPERT_CHUNK):
            x = xs_ref[r0:r0 + EXPERT_CHUNK].reshape(EXPERT_CHUNK, D).astype(bf16)
            gu = jnp.dot(x, wgu_s[...], preferred_element_type=f32) + bgu_ref[...]
            gate = jnp.minimum(gu[:, :D], SWIGLU_LIMIT)
            up = jnp.clip(gu[:, D:], -SWIGLU_LIMIT, SWIGLU_LIMIT)
            glu = gate * jax.nn.sigmoid(gate * SWIGLU_ALPHA)
            a = ((up + 1.0) * glu).astype(bf16)
            y = jnp.dot(a, wd_s[...], preferred_element_type=f32) + bd_ref[...]
            y_ref[r0:r0 + EXPERT_CHUNK] = y.reshape(EXPERT_CHUNK, SUB, LANE)

    @pl.when(bv_ref[b] == 0)
    def _():
        y_ref[...] = jnp.zeros_like(y_ref)


def _experts(xs, blk_e, blk_valid, blk_first, w_gu, b_gu, w_down, b_down, layer):
    n_rows = xs.shape[0]
    n_blocks = n_rows // EXPERT_BLOCK
    grid_spec = pltpu.PrefetchScalarGridSpec(
        num_scalar_prefetch=3,
        grid=(n_blocks,),
        in_specs=[pl.BlockSpec((EXPERT_BLOCK, SUB, LANE), lambda b, be, bv, bf: (b, 0, 0)),
                  pl.BlockSpec((None, None, D, 2 * D), lambda b, be, bv, bf: (layer, be[b], 0, 0)),
                  pl.BlockSpec((None, None, 1, 2 * D), lambda b, be, bv, bf: (layer, be[b], 0, 0)),
                  pl.BlockSpec((None, None, D, D), lambda b, be, bv, bf: (layer, be[b], 0, 0)),
                  pl.BlockSpec((None, None, 1, D), lambda b, be, bv, bf: (layer, be[b], 0, 0))],
        out_specs=pl.BlockSpec((EXPERT_BLOCK, SUB, LANE), lambda b, be, bv, bf: (b, 0, 0)),
        scratch_shapes=[pltpu.VMEM((D, 2 * D), bf16), pltpu.VMEM((D, D), bf16)],
    )
    return pl.pallas_call(
        _experts_kernel,
        grid_spec=grid_spec,
        out_shape=jax.ShapeDtypeStruct((n_rows, SUB, LANE), f32),
        compiler_params=_params("arbitrary"),
        name="experts",
    )(blk_e, blk_valid, blk_first, xs, w_gu, b_gu[:, :, None, :], w_down, b_down[:, :, None, :])


def _combine_kernel(nseg_ref, pos_ref, off_ref, slots_hbm, gates_hbm, ys_hbm, x1_ref, gate_ref, g_ref, o_ref,
                    slot_s, gate_s, ybuf, mbuf, sem_s, sem):
    i = pl.program_id(0)
    last = pl.num_programs(0) - 1
    buf = i % 2
    t4 = TOP_K * MOE_TILE
    cps = pltpu.make_async_copy(slots_hbm.at[pl.ds(i * t4, t4)], slot_s, sem_s.at[0])
    cpg = pltpu.make_async_copy(gates_hbm.at[pl.ds(i * t4, t4)], gate_s, sem_s.at[1])
    cps.start()
    cpg.start()

    def seg(tile, b, e):
        j = tile * N_EXPERTS + e
        return _row_copies(ys_hbm, pos_ref[j], ybuf.at[b], off_ref[j], nseg_ref[j], sem.at[b], MOE_TILE_BITS)

    def for_experts(fn):
        def body(e, carry):
            fn(e)
            return carry
        lax.fori_loop(0, N_EXPERTS, body, 0)

    @pl.when(i == 0)
    def _():
        for_experts(lambda e: _start_all(seg(i, buf, e)))

    @pl.when(i < last)
    def _():
        for_experts(lambda e: _start_all(seg(i + 1, 1 - buf, e)))

    cps.wait()
    cpg.wait()
    for_experts(lambda e: _wait_all(seg(i, buf, e)))
    ybuf_b = ybuf.at[buf]

    def tok(tb, carry):
        for u in range(8):
            t = tb * 8 + u
            acc = gate_s[t * TOP_K] * ybuf_b[slot_s[t * TOP_K]]
            for k in range(1, TOP_K):
                acc = acc + gate_s[t * TOP_K + k] * ybuf_b[slot_s[t * TOP_K + k]]
            mbuf[t] = acc
        return carry

    lax.fori_loop(0, MOE_TILE // 8, tok, 0)
    for r0 in range(0, MOE_TILE, EXPERT_BLOCK):
        m = mbuf[r0:r0 + EXPERT_BLOCK].reshape(EXPERT_BLOCK, D)
        o_ref[r0:r0 + EXPERT_BLOCK, :] = (x1_ref[r0:r0 + EXPERT_BLOCK, :]
                                          + gate_ref[...] * _rms(m, g_ref[...]))


def _combine(ys, slots, gates, nseg, pos, off, x1, gate, g, tiles_per_batch):
    n_tok = x1.shape[0]
    nt = n_tok // MOE_TILE
    grid_spec = pltpu.PrefetchScalarGridSpec(
        num_scalar_prefetch=3,
        grid=(nt,),
        in_specs=[pl.BlockSpec(memory_space=pl.ANY),
                  pl.BlockSpec(memory_space=pl.ANY),
                  pl.BlockSpec(memory_space=pl.ANY),
                  pl.BlockSpec((MOE_TILE, D), lambda i, *_: (i, 0)),
                  pl.BlockSpec((None, 1, D), lambda i, *_: (i // tiles_per_batch, 0, 0)),
                  pl.BlockSpec((1, D), lambda i, *_: (0, 0))],
        out_specs=pl.BlockSpec((MOE_TILE, D), lambda i, *_: (i, 0)),
        scratch_shapes=[pltpu.SMEM((TOP_K * MOE_TILE,), jnp.int32),
                        pltpu.SMEM((TOP_K * MOE_TILE,), f32),
                        pltpu.VMEM((2, TOP_K * MOE_TILE, SUB, LANE), f32),
                        pltpu.VMEM((MOE_TILE, SUB, LANE), f32),
                        pltpu.SemaphoreType.DMA((2,)),
                        pltpu.SemaphoreType.DMA((2,))],
    )
    return pl.pallas_call(
        _combine_kernel,
        grid_spec=grid_spec,
        out_shape=jax.ShapeDtypeStruct((n_tok, D), f32),
        compiler_params=_params("arbitrary"),
        name="moe_combine",
    )(nseg, pos, off, slots, gates, ys, x1, gate, g)


def _moe_tables(top_i):
    n_tok = top_i.shape[0]
    nt = n_tok // MOE_TILE
    oh4 = (top_i[:, :, None] == jnp.arange(N_EXPERTS)[None, None, :]).astype(f32)
    oh = oh4.sum(axis=1).reshape(nt, MOE_TILE, N_EXPERTS)
    tri = (jnp.arange(MOE_TILE)[:, None] > jnp.arange(MOE_TILE)[None, :]).astype(bf16)
    rank = jnp.einsum('ab,nbe->nae', tri, oh.astype(bf16), preferred_element_type=f32)
    nseg = oh.sum(axis=1).astype(jnp.int32)
    counts = nseg.sum(axis=0)
    pcounts = (counts + EXPERT_BLOCK - 1) // EXPERT_BLOCK * EXPERT_BLOCK
    pend = jnp.cumsum(pcounts)
    pstart = pend - pcounts
    pos = pstart[None, :] + jnp.cumsum(nseg, axis=0) - nseg
    off = jnp.cumsum(nseg, axis=1) - nseg
    base = (off.astype(f32)[:, None, :] + rank).reshape(n_tok, 1, N_EXPERTS)
    slots = jnp.sum(oh4 * base, axis=-1).astype(jnp.int32).reshape(-1)
    n_blocks = -(-(n_tok * TOP_K) // EXPERT_BLOCK) + N_EXPERTS
    blk_start = jnp.arange(n_blocks, dtype=jnp.int32) * EXPERT_BLOCK
    blk_e = jnp.minimum(jnp.sum(blk_start[:, None] >= pend[None, :], axis=1), N_EXPERTS - 1).astype(jnp.int32)
    blk_valid = (blk_start < pend[-1]).astype(jnp.int32)
    blk_first = jnp.concatenate([jnp.ones((1,), jnp.int32), (blk_e[1:] != blk_e[:-1]).astype(jnp.int32)])
    tabs = dict(nseg=nseg.reshape(-1), pos=pos.reshape(-1).astype(jnp.int32), off=off.reshape(-1),
                padn=(pcounts - counts).astype(jnp.int32),
                padpos=jnp.concatenate([pstart + counts, pend[-1:]]).astype(jnp.int32),
                slots=slots, blk_e=blk_e, blk_valid=blk_valid, blk_first=blk_first)
    return tabs, n_blocks * EXPERT_BLOCK


def _moe_residual(h2t, top_i, top_g, x1, gate, g, w_gu, b_gu, w_down, b_down, layer, tiles_per_batch):
    t, n_rows = _moe_tables(top_i)
    xs = _dispatch(h2t, t["slots"], t["nseg"], t["pos"], t["off"], t["padn"], t["padpos"], n_rows)
    ys = _experts(xs, t["blk_e"], t["blk_valid"], t["blk_first"], w_gu, b_gu, w_down, b_down, layer)
    return _combine(ys, t["slots"], top_g.reshape(-1), t["nseg"], t["pos"], t["off"], x1, gate, g,
                    tiles_per_batch)


def kernel(x, c, ctx, c_ctx, w_ada, b_ada, norm_g, w_qkv, w_o, rpb, w_pw1, b_pw1, w_dw, b_dw, cn_g, cn_b,
           w_pw2, b_pw2, w_router, b_router, w_gu, b_gu, w_down, b_down):
    bsz, seq, _ = x.shape
    tm = min(512, seq)
    cvec = jnp.zeros((8, D), f32).at[:bsz].set(c).at[bsz].set(c_ctx)
    mod = _ada(cvec, w_ada, b_ada)

    def lat(i, k):
        return mod[i, :bsz, k * D:(k + 1) * D][:, None, :]

    def cx(i, k):
        return jnp.broadcast_to(mod[i, bsz, k * D:(k + 1) * D][None, None, :], (bsz, 1, D))

    row = lambda v: v.reshape(1, -1)
    zero_b = jnp.zeros((1, 3 * D), f32)

    wq = w_qkv[0].astype(bf16)
    qkv = _prenorm_matmul(x, row(norm_g[0, 0]), lat(0, 0), lat(0, 1), wq, zero_b,
                          mode="qkv", tm=tm, out_dtype=bf16)
    kv_ctx = _prenorm_matmul(ctx, row(norm_g[0, 0]), cx(0, 0), cx(0, 1), wq, zero_b,
                             mode="qkv", tm=ctx.shape[1], out_dtype=bf16)
    attn = _na_attention(qkv, kv_ctx, _na_bias_table(rpb[0]))
    epi = (x, lat(0, 2), row(norm_g[0, 1]), row(norm_g[0, 2]), lat(0, 3), lat(0, 4),
           w_router[0].astype(bf16), row(b_router[0]))
    x1, h2, ti, tg = _attn_out(attn, w_o[0].astype(bf16), epi, tm=tm)
    x = _moe_residual(h2.reshape(-1, SUB, LANE), ti.reshape(-1, TOP_K), tg.reshape(-1, TOP_K),
                      x1.reshape(-1, D), lat(0, 5), row(norm_g[0, 3]),
                      w_gu, b_gu, w_down, b_down, 0, seq // MOE_TILE).reshape(bsz, seq, D)

    u = _prenorm_matmul(x, row(norm_g[1, 0]), lat(1, 0), lat(1, 1), w_pw1[0].astype(bf16), row(b_pw1[0]),
                        mode="glu", tm=tm, out_dtype=f32)
    epi = (x, lat(1, 2), row(norm_g[1, 1]), row(norm_g[1, 2]), lat(1, 3), lat(1, 4),
           w_router[1].astype(bf16), row(b_router[1]))
    x1, h2, ti, tg = _conv_tail(u, w_dw[0], row(b_dw[0]), row(cn_g[0]), row(cn_b[0]),
                                w_pw2[0].astype(bf16), row(b_pw2[0]), epi, tm=min(256, seq))
    return _moe_residual(h2.reshape(-1, SUB, LANE), ti.reshape(-1, TOP_K), tg.reshape(-1, TOP_K),
                         x1.reshape(-1, D), lat(1, 5), row(norm_g[1, 3]),
                         w_gu, b_gu, w_down, b_down, 1, seq // MOE_TILE).reshape(bsz, seq, D)
```

```python
import functools

import jax
import jax.numpy as jnp
from jax import lax
from jax.experimental import pallas as pl
from jax.experimental.pallas import tpu as pltpu

D = 1024
GRID_W = 64
N_HEADS = 16
HEAD_DIM = 64
HEAD_PAIRS = N_HEADS // 2
WIN_ROWS = 8
WIN_COLS = 16
CONV_W = 31
CONV_HALO = 16
N_EXPERTS = 32
TOP_K = 4
SWIGLU_LIMIT = 7.0
SWIGLU_ALPHA = 1.702
EXPERT_BLOCK = 512
EPS = 1e-6
NEG = -1e30
NA_CHUNK_ROWS = 32
NA_HALO_ROWS = WIN_ROWS // 2
LOG2E = 1.4426950408889634
Q_SCALE = HEAD_DIM ** -0.5 * LOG2E
LANE = 128
SUB = 8
MOE_TILE = 1024
COMBINE_CHUNK = 256
MOE_TILE_BITS = MOE_TILE.bit_length() - 1
PAD_BITS = EXPERT_BLOCK.bit_length() - 2
V7X_VMEM_LIMIT = 56 * 1024 * 1024

f32 = jnp.float32
bf16 = jnp.bfloat16


def _params(*sem):
    return pltpu.CompilerParams(dimension_semantics=sem, vmem_limit_bytes=V7X_VMEM_LIMIT)


def _rms(v, g):
    return v * lax.rsqrt(jnp.mean(v * v, axis=-1, keepdims=True) + EPS) * g


def _ada_kernel(c_ref, w_ref, b_ref, o_ref):
    c = c_ref[...]
    s = c * jax.nn.sigmoid(c)
    o_ref[...] = jnp.dot(s, w_ref[...], preferred_element_type=f32,
                         precision=lax.Precision.HIGHEST) + b_ref[...]


def _ada(cvec, w_ada, b_ada):
    depth, _, n = w_ada.shape
    tn = 1536
    return pl.pallas_call(
        _ada_kernel,
        grid=(depth, n // tn),
        in_specs=[pl.BlockSpec((8, D), lambda i, j: (0, 0)),
                  pl.BlockSpec((None, D, tn), lambda i, j: (i, 0, j)),
                  pl.BlockSpec((None, 1, tn), lambda i, j: (i, 0, j))],
        out_specs=pl.BlockSpec((None, 8, tn), lambda i, j: (i, 0, j)),
        out_shape=jax.ShapeDtypeStruct((depth, 8, n), f32),
        compiler_params=_params("arbitrary", "arbitrary"),
        name="ada",
    )(cvec, w_ada, b_ada.reshape(depth, 1, n))


def _prenorm_matmul_kernel(x_ref, g_ref, sh_ref, sc_ref, w_ref, b_ref, o_ref, *, mode, tn):
    h = _rms(x_ref[...], g_ref[...]) * (1.0 + sc_ref[...]) + sh_ref[...]
    hb = h.astype(bf16)
    n_out = o_ref.shape[-1]
    for j in range(n_out // tn):
        if mode == "glu":
            a = jnp.dot(hb, w_ref[:, j * tn:(j + 1) * tn], preferred_element_type=f32) + b_ref[:, j * tn:(j + 1) * tn]
            g = (jnp.dot(hb, w_ref[:, n_out + j * tn:n_out + (j + 1) * tn], preferred_element_type=f32)
                 + b_ref[:, n_out + j * tn:n_out + (j + 1) * tn])
            o_ref[:, j * tn:(j + 1) * tn] = (a * jax.nn.sigmoid(g)).astype(o_ref.dtype)
        else:
            a = jnp.dot(hb, w_ref[:, j * tn:(j + 1) * tn], preferred_element_type=f32)
            if mode == "qkv" and (j + 1) * tn <= D:
                a = a * Q_SCALE
            o_ref[:, j * tn:(j + 1) * tn] = a.astype(o_ref.dtype)


def _prenorm_matmul(x, g, sh, sc, w, b, *, mode, tm, out_dtype):
    bsz, seq, _ = x.shape
    n = w.shape[1]
    n_out = n // 2 if mode == "glu" else n
    tn = 512
    kern = functools.partial(_prenorm_matmul_kernel, mode=mode, tn=tn)
    return pl.pallas_call(
        kern,
        grid=(bsz, seq // tm),
        in_specs=[pl.BlockSpec((None, tm, D), lambda b_, t: (b_, t, 0)),
                  pl.BlockSpec((1, D), lambda b_, t: (0, 0)),
                  pl.BlockSpec((None, 1, D), lambda b_, t: (b_, 0, 0)),
                  pl.BlockSpec((None, 1, D), lambda b_, t: (b_, 0, 0)),
                  pl.BlockSpec((D, n), lambda b_, t: (0, 0)),
                  pl.BlockSpec((1, n), lambda b_, t: (0, 0))],
        out_specs=pl.BlockSpec((None, tm, n_out), lambda b_, t: (b_, t, 0)),
        out_shape=jax.ShapeDtypeStruct((bsz, seq, n_out), out_dtype),
        compiler_params=_params("parallel", "parallel"),
        name="prenorm_matmul_" + mode,
    )(x, g, sh, sc, w, b)


def _na_kernel(q_ref, kp_ref, kc_ref, kn_ref, vp_ref, vc_ref, vn_ref, kx_ref, vx_ref, bias_ref, o_ref,
               k_buf, v_buf, vx_buf, *, rows, chunk_rows):
    c = pl.program_id(2)
    blk = chunk_rows * GRID_W
    halo = NA_HALO_ROWS * GRID_W
    hp2 = 2 * HEAD_DIM
    k_buf[0:halo] = kp_ref[...]
    k_buf[halo:halo + blk] = kc_ref[...]
    k_buf[halo + blk:] = kn_ref[...]
    v_buf[0:halo, 0:hp2] = vp_ref[...]
    v_buf[halo:halo + blk, 0:hp2] = vc_ref[...]
    v_buf[halo + blk:, 0:hp2] = vn_ref[...]
    v_buf[:, hp2:] = jnp.ones((blk + 2 * halo, hp2), bf16)
    vx_buf[:, 0:hp2] = vx_ref[...]
    vx_buf[:, hp2:] = jnp.ones((vx_buf.shape[0], hp2), bf16)
    lane = lax.broadcasted_iota(jnp.int32, (GRID_W, hp2), 1)
    first_head = lane < HEAD_DIM
    nt = (((1,), (1,)), ((), ()))
    tn = (((0,), (0,)), ((), ()))
    win = WIN_ROWS * GRID_W

    for i in range(chunk_rows):
        r = c * chunk_rows + i
        rs = jnp.clip(r - WIN_ROWS // 2, 0, rows - WIN_ROWS)
        delta = r - rs
        start = pl.multiple_of((rs - c * chunk_rows + NA_HALO_ROWS) * GRID_W, GRID_W)
        q = q_ref[i * GRID_W:(i + 1) * GRID_W, :]
        zero = jnp.zeros_like(q)
        qbd = jnp.concatenate([jnp.where(first_head, q, zero), jnp.where(first_head, zero, q)], axis=0)
        s_lat = lax.dot_general(k_buf[pl.ds(start, win), :], qbd, nt, preferred_element_type=f32) + bias_ref[delta]
        s_ctx = lax.dot_general(kx_ref[...], qbd, nt, preferred_element_type=f32)
        m = jnp.maximum(jnp.max(s_lat, axis=0, keepdims=True), jnp.max(s_ctx, axis=0, keepdims=True))
        p_lat = jnp.exp2(s_lat - m).astype(bf16)
        p_ctx = jnp.exp2(s_ctx - m).astype(bf16)
        o = (lax.dot_general(p_lat, v_buf[pl.ds(start, win), :], tn, preferred_element_type=f32)
             + lax.dot_general(p_ctx, vx_buf[...], tn, preferred_element_type=f32))
        o = o[:, 0:hp2] * (1.0 / o[:, hp2:hp2 + 1])
        o_ref[i * GRID_W:(i + 1) * GRID_W, :] = jnp.where(
            first_head, o[0:GRID_W], o[GRID_W:2 * GRID_W]).astype(o_ref.dtype)


def _na_bias_table(rpb):
    qc = jnp.arange(GRID_W)
    cs = jnp.clip(qc - WIN_COLS // 2, 0, GRID_W - WIN_COLS)
    kc = jnp.arange(GRID_W)
    inside = (kc[None, :] >= cs[:, None]) & (kc[None, :] < cs[:, None] + WIN_COLS)
    col_off = kc[None, :] - qc[:, None] + (WIN_COLS - 1)
    col_sel = ((col_off[:, :, None] == jnp.arange(2 * WIN_COLS - 1)) & inside[:, :, None]).astype(f32)
    t = jnp.einsum('hrc,qkc->hrqk', rpb.astype(f32), col_sel, precision=lax.Precision.HIGHEST)
    t = jnp.where(inside[None, None], t * LOG2E, NEG)
    t = jnp.stack([t[:, WIN_ROWS - 1 - d:2 * WIN_ROWS - 1 - d] for d in range(WIN_ROWS)], axis=1)
    t = jnp.transpose(t, (1, 2, 4, 0, 3))
    t = t.reshape(WIN_ROWS, WIN_ROWS * GRID_W, HEAD_PAIRS, 2 * GRID_W)
    return jnp.transpose(t, (2, 0, 1, 3))


def _na_attention(qkv, kv_ctx, bias_t):
    bsz, seq, _ = qkv.shape
    ctx_len = kv_ctx.shape[1]
    rows = seq // GRID_W
    chunk_rows = min(NA_CHUNK_ROWS, rows)
    blk = chunk_rows * GRID_W
    halo = NA_HALO_ROWS * GRID_W
    n_chunks = seq // blk
    per = blk // halo
    n_halo = seq // halo
    hp2 = 2 * HEAD_DIM
    kblk = D // hp2
    vblk = 2 * D // hp2

    def cur(col0):
        return pl.BlockSpec((None, blk, hp2), lambda b, h, c: (b, c, col0 + h))

    def before(col0):
        return pl.BlockSpec((None, halo, hp2), lambda b, h, c: (b, jnp.maximum(c * per - 1, 0), col0 + h))

    def after(col0):
        return pl.BlockSpec((None, halo, hp2), lambda b, h, c: (b, jnp.minimum((c + 1) * per, n_halo - 1), col0 + h))

    return pl.pallas_call(
        functools.partial(_na_kernel, rows=rows, chunk_rows=chunk_rows),
        grid=(bsz, HEAD_PAIRS, n_chunks),
        in_specs=[cur(0),
                  before(kblk), cur(kblk), after(kblk),
                  before(vblk), cur(vblk), after(vblk),
                  pl.BlockSpec((None, ctx_len, hp2), lambda b, h, c: (b, 0, kblk + h)),
                  pl.BlockSpec((None, ctx_len, hp2), lambda b, h, c: (b, 0, vblk + h)),
                  pl.BlockSpec((None, WIN_ROWS, WIN_ROWS * GRID_W, hp2), lambda b, h, c: (h, 0, 0, 0))],
        out_specs=pl.BlockSpec((None, blk, hp2), lambda b, h, c: (b, c, h)),
        out_shape=jax.ShapeDtypeStruct((bsz, seq, D), bf16),
        scratch_shapes=[pltpu.VMEM((blk + 2 * halo, hp2), bf16), pltpu.VMEM((blk + 2 * halo, 2 * hp2), bf16),
                        pltpu.VMEM((ctx_len, 2 * hp2), bf16)],
        compiler_params=_params("parallel", "parallel", "arbitrary"),
        name="na_attention",
    )(qkv, qkv, qkv, qkv, qkv, qkv, qkv, kv_ctx, kv_ctx, bias_t)


def _mixer_epilogue(y, x_ref, gate_ref, g_post_ref, g_pre_ref, sh_ref, sc_ref, wr_ref, br_ref,
                    x1_ref, h2_ref, ti_ref, tg_ref):
    x1 = x_ref[...] + gate_ref[...] * _rms(y, g_post_ref[...])
    x1_ref[...] = x1
    h2 = _rms(x1, g_pre_ref[...]) * (1.0 + sc_ref[...]) + sh_ref[...]
    h2_ref[...] = h2.reshape(h2_ref.shape)
    hb = h2.astype(bf16)
    logits = jnp.dot(hb, wr_ref[...], preferred_element_type=f32) + br_ref[...]
    eidx = lax.broadcasted_iota(jnp.int32, logits.shape, 1).astype(f32)
    work = logits
    vals, idxs = [], []
    for _ in range(TOP_K):
        m = jnp.max(work, axis=-1, keepdims=True)
        first = jnp.min(jnp.where(work == m, eidx, float(N_EXPERTS)), axis=-1, keepdims=True)
        vals.append(m)
        idxs.append(first)
        work = jnp.where(eidx == first, NEG, work)
    top_v = jnp.concatenate(vals, axis=-1)
    e = jnp.exp(top_v - vals[0])
    tg_ref[...] = e / jnp.sum(e, axis=-1, keepdims=True)
    ti_ref[...] = jnp.concatenate(idxs, axis=-1).astype(jnp.int32)


def _attn_out_kernel(a_ref, wo_ref, *rest):
    y = jnp.dot(a_ref[...], wo_ref[...], preferred_element_type=f32)
    _mixer_epilogue(y, *rest)


def _epilogue_specs(tm):
    tok = lambda b, t: (b, t, 0)
    per_b = lambda b, t: (b, 0, 0)
    const = lambda b, t: (0, 0)
    in_specs = [pl.BlockSpec((None, tm, D), tok),
                pl.BlockSpec((None, 1, D), per_b),
                pl.BlockSpec((1, D), const),
                pl.BlockSpec((1, D), const),
                pl.BlockSpec((None, 1, D), per_b),
                pl.BlockSpec((None, 1, D), per_b),
                pl.BlockSpec((D, N_EXPERTS), const),
                pl.BlockSpec((1, N_EXPERTS), const)]
    out_specs = [pl.BlockSpec((None, tm, D), tok),
                 pl.BlockSpec((None, tm, SUB, LANE), lambda b, t: (b, t, 0, 0)),
                 pl.BlockSpec((None, tm, TOP_K), tok),
                 pl.BlockSpec((None, tm, TOP_K), tok)]
    return in_specs, out_specs


def _epilogue_shapes(bsz, seq):
    return [jax.ShapeDtypeStruct((bsz, seq, D), f32),
            jax.ShapeDtypeStruct((bsz, seq, SUB, LANE), f32),
            jax.ShapeDtypeStruct((bsz, seq, TOP_K), jnp.int32),
            jax.ShapeDtypeStruct((bsz, seq, TOP_K), f32)]


def _attn_out(a, wo, epi_args, *, tm):
    bsz, seq, _ = a.shape
    in_specs, out_specs = _epilogue_specs(tm)
    return pl.pallas_call(
        _attn_out_kernel,
        grid=(bsz, seq // tm),
        in_specs=[pl.BlockSpec((None, tm, D), lambda b, t: (b, t, 0)),
                  pl.BlockSpec((D, D), lambda b, t: (0, 0))] + in_specs,
        out_specs=out_specs,
        out_shape=_epilogue_shapes(bsz, seq),
        compiler_params=_params("parallel", "parallel"),
        name="attn_out",
    )(a, wo, *epi_args)


def _conv_kernel(up_ref, uc_ref, un_ref, wdw_ref, bdw_ref, lg_ref, lb_ref, w2_ref, b2_ref, *rest, tm, sub):
    ext, shifted = rest[-2:]
    t = pl.program_id(1)
    nt = pl.num_programs(1)
    ext[0:CONV_HALO] = jnp.where(t > 0, up_ref[...], 0.0)
    ext[CONV_HALO:CONV_HALO + tm] = uc_ref[...]
    ext[CONV_HALO + tm:] = jnp.where(t < nt - 1, un_ref[...], 0.0)
    span = tm + 2 * CONV_HALO - SUB
    for a in range(1, SUB):
        shifted[a - 1, 0:span] = ext[a:a + span]
    base = CONV_HALO - CONV_W // 2
    parts = []
    for rc in range(tm // sub):
        acc = jnp.zeros((sub, D), f32) + bdw_ref[...]
        for k in range(CONV_W):
            a, r0 = (base + k) % SUB, rc * sub + (base + k) // SUB * SUB
            win = ext[r0:r0 + sub, :] if a == 0 else shifted[a - 1, r0:r0 + sub, :]
            acc = acc + win * wdw_ref[k:k + 1, :]
        mu = jnp.mean(acc, axis=-1, keepdims=True)
        d = acc - mu
        var = jnp.mean(d * d, axis=-1, keepdims=True)
        z = d * lax.rsqrt(var + EPS) * lg_ref[...] + lb_ref[...]
        parts.append((z * jax.nn.sigmoid(z)).astype(bf16))
    z = jnp.concatenate(parts, axis=0)
    y = jnp.dot(z, w2_ref[...], preferred_element_type=f32) + b2_ref[...]
    _mixer_epilogue(y, *rest[:-2])


def _conv_tail(u, w_dw, b_dw, ln_g, ln_b, w2, b2, epi_args, *, tm):
    bsz, seq, _ = u.shape
    in_specs, out_specs = _epilogue_specs(tm)
    per = tm // CONV_HALO
    n_halo = seq // CONV_HALO
    const = lambda b, t: (0, 0)
    return pl.pallas_call(
        functools.partial(_conv_kernel, tm=tm, sub=8),
        grid=(bsz, seq // tm),
        in_specs=[pl.BlockSpec((None, CONV_HALO, D), lambda b, t: (b, jnp.maximum(t * per - 1, 0), 0)),
                  pl.BlockSpec((None, tm, D), lambda b, t: (b, t, 0)),
                  pl.BlockSpec((None, CONV_HALO, D), lambda b, t: (b, jnp.minimum((t + 1) * per, n_halo - 1), 0)),
                  pl.BlockSpec((CONV_W, D), const),
                  pl.BlockSpec((1, D), const),
                  pl.BlockSpec((1, D), const),
                  pl.BlockSpec((1, D), const),
                  pl.BlockSpec((D, D), const),
                  pl.BlockSpec((1, D), const)] + in_specs,
        out_specs=out_specs,
        out_shape=_epilogue_shapes(bsz, seq),
        scratch_shapes=[pltpu.VMEM((tm + 2 * CONV_HALO, D), f32),
                        pltpu.VMEM((SUB - 1, tm + 2 * CONV_HALO, D), f32)],
        compiler_params=_params("parallel", "arbitrary"),
        name="conv_tail",
    )(u, u, u, w_dw, b_dw, ln_g, ln_b, w2, b2, *epi_args)


def _row_copies(src, src_off, dst, dst_off, n, sem, max_bit, fixed_src=False):
    out = []
    for b in range(max_bit, -1, -1):
        size = SUB << b
        start = (n >> (b + 1)) << (b + 1)
        s_off = src_off if fixed_src else src_off + start
        cp = pltpu.make_async_copy(src.at[pl.ds(pl.multiple_of(s_off * SUB, SUB), size)],
                                   dst.at[pl.ds(pl.multiple_of((dst_off + start) * SUB, SUB), size)], sem)
        out.append((((n >> b) & 1) == 1, cp))
    return out


def _start_all(copies):
    for cond, cp in copies:
        pl.when(cond)(cp.start)


def _wait_all(copies):
    for cond, cp in copies:
        pl.when(cond)(cp.wait)


def _dispatch_kernel(nseg_ref, pos_ref, off_ref, padn_ref, padpos_ref, slots_hbm, x_ref, xs_hbm,
                     slot_s, stage, zeros, sem_s, sem):
    i = pl.program_id(0)
    last = pl.num_programs(0) - 1
    buf = i % 2
    t4 = TOP_K * MOE_TILE
    cp = pltpu.make_async_copy(slots_hbm.at[pl.ds(i * t4, t4)], slot_s, sem_s)
    cp.start()
    cp.wait()
    stage_b = stage.at[buf]

    def tok(tb, carry):
        for u in range(4):
            t = tb * 4 + u
            v = x_ref[t]
            for k in range(TOP_K):
                stage_b[pl.ds(pl.multiple_of(slot_s[t * TOP_K + k], SUB), SUB), :] = v
        return carry

    lax.fori_loop(0, MOE_TILE // 4, tok, 0)

    def seg(tile, b, e):
        j = tile * N_EXPERTS + e
        return _row_copies(stage.at[b], off_ref[j], xs_hbm, pos_ref[j], nseg_ref[j], sem.at[b], MOE_TILE_BITS)

    def for_experts(fn):
        def body(e, carry):
            fn(e)
            return carry
        lax.fori_loop(0, N_EXPERTS, body, 0)

    for_experts(lambda e: _start_all(seg(i, buf, e)))

    @pl.when(i > 0)
    def _():
        for_experts(lambda e: _wait_all(seg(i - 1, 1 - buf, e)))

    @pl.when(i == last)
    def _():
        for_experts(lambda e: _wait_all(seg(i, buf, e)))
        zeros[...] = jnp.zeros_like(zeros)
        zsem = sem.at[0]

        def pad(e):
            return _row_copies(zeros, 0, xs_hbm, padpos_ref[e], padn_ref[e], zsem, PAD_BITS, fixed_src=True)

        for_experts(lambda e: _start_all(pad(e)))
        for_experts(lambda e: _wait_all(pad(e)))

        tail0 = padpos_ref[N_EXPERTS] * SUB
        zrows = zeros.shape[0]

        def tail(j):
            return pltpu.make_async_copy(zeros, xs_hbm.at[pl.ds(pl.multiple_of(tail0 + j * zrows, SUB), zrows)], zsem)

        def tail_start(j, carry):
            tail(j).start()
            return carry

        def tail_wait(j, carry):
            tail(j).wait()
            return carry

        n_tail = (xs_hbm.shape[0] - tail0) // zrows
        lax.fori_loop(0, n_tail, tail_start, 0)
        lax.fori_loop(0, n_tail, tail_wait, 0)


def _dispatch(h2t, slots, nseg, pos, off, padn, padpos, n_rows):
    n_tok = h2t.shape[0]
    nt = n_tok // MOE_TILE
    grid_spec = pltpu.PrefetchScalarGridSpec(
        num_scalar_prefetch=5,
        grid=(nt,),
        in_specs=[pl.BlockSpec(memory_space=pl.ANY),
                  pl.BlockSpec((MOE_TILE, SUB, LANE), lambda i, *_: (i, 0, 0))],
        out_specs=pl.BlockSpec(memory_space=pl.ANY),
        scratch_shapes=[pltpu.SMEM((TOP_K * MOE_TILE,), jnp.int32),
                        pltpu.VMEM((2, TOP_K * MOE_TILE * SUB, LANE), f32),
                        pltpu.VMEM((SUB << PAD_BITS, LANE), f32),
                        pltpu.SemaphoreType.DMA(()),
                        pltpu.SemaphoreType.DMA((2,))],
    )
    return pl.pallas_call(
        _dispatch_kernel,
        grid_spec=grid_spec,
        out_shape=jax.ShapeDtypeStruct((n_rows * SUB, LANE), f32),
        compiler_params=_params("arbitrary"),
        name="moe_dispatch",
    )(nseg, pos, off, padn, padpos, slots, h2t)


def _experts_kernel(be_ref, bv_ref, bf_ref, xs_ref, wgu_ref, bgu_ref, wd_ref, bd_ref, y_ref, wgu_s, wd_s):
    b = pl.program_id(0)

    @pl.when(bf_ref[b] == 1)
    def _():
        wgu_s[...] = wgu_ref[...].astype(bf16)
        wd_s[...] = wd_ref[...].astype(bf16)

    @pl.when(bv_ref[b] == 1)
    def _():
        x = xs_ref[...].reshape(EXPERT_BLOCK, D).astype(bf16)
        gu = jnp.dot(x, wgu_s[...], preferred_element_type=f32) + bgu_ref[...]
        gate = jnp.minimum(gu[:, :D], SWIGLU_LIMIT)
        up = jnp.clip(gu[:, D:], -SWIGLU_LIMIT, SWIGLU_LIMIT)
        glu = gate * jax.nn.sigmoid(gate * SWIGLU_ALPHA)
        a = ((up + 1.0) * glu).astype(bf16)
        y = jnp.dot(a, wd_s[...], preferred_element_type=f32) + bd_ref[...]
        y_ref[...] = y.reshape(EXPERT_BLOCK, SUB, LANE)

    @pl.when(bv_ref[b] == 0)
    def _():
        y_ref[...] = jnp.zeros_like(y_ref)


def _experts(xs, blk_e, blk_valid, blk_first, w_gu, b_gu, w_down, b_down, layer):
    n_rows = xs.shape[0]
    n_blocks = n_rows // EXPERT_BLOCK
    grid_spec = pltpu.PrefetchScalarGridSpec(
        num_scalar_prefetch=3,
        grid=(n_blocks,),
        in_specs=[pl.BlockSpec((EXPERT_BLOCK, SUB, LANE), lambda b, be, bv, bf: (b, 0, 0)),
                  pl.BlockSpec((None, None, D, 2 * D), lambda b, be, bv, bf: (layer, be[b], 0, 0)),
                  pl.BlockSpec((None, None, 1, 2 * D), lambda b, be, bv, bf: (layer, be[b], 0, 0)),
                  pl.BlockSpec((None, None, D, D), lambda b, be, bv, bf: (layer, be[b], 0, 0)),
                  pl.BlockSpec((None, None, 1, D), lambda b, be, bv, bf: (layer, be[b], 0, 0))],
        out_specs=pl.BlockSpec((EXPERT_BLOCK, SUB, LANE), lambda b, be, bv, bf: (b, 0, 0)),
        scratch_shapes=[pltpu.VMEM((D, 2 * D), bf16), pltpu.VMEM((D, D), bf16)],
    )
    return pl.pallas_call(
        _experts_kernel,
        grid_spec=grid_spec,
        out_shape=jax.ShapeDtypeStruct((n_rows, SUB, LANE), f32),
        compiler_params=_params("arbitrary"),
        name="experts",
    )(blk_e, blk_valid, blk_first, xs, w_gu, b_gu[:, :, None, :], w_down, b_down[:, :, None, :])


def _combine_kernel(nseg_ref, pos_ref, off_ref, slots_hbm, gates_hbm, ys_hbm, x1_ref, gate_ref, g_ref, o_ref,
                    slot_s, gate_s, ybuf, mbuf, sem_s, sem):
    i = pl.program_id(0)
    last = pl.num_programs(0) - 1
    buf = i % 2
    t4 = TOP_K * MOE_TILE
    cps = pltpu.make_async_copy(slots_hbm.at[pl.ds(i * t4, t4)], slot_s, sem_s.at[0])
    cpg = pltpu.make_async_copy(gates_hbm.at[pl.ds(i * t4, t4)], gate_s, sem_s.at[1])
    cps.start()
    cpg.start()

    def seg(tile, b, e):
        j = tile * N_EXPERTS + e
        return _row_copies(ys_hbm, pos_ref[j], ybuf.at[b], off_ref[j], nseg_ref[j], sem.at[b], MOE_TILE_BITS)

    def for_experts(fn):
        def body(e, carry):
            fn(e)
            return carry
        lax.fori_loop(0, N_EXPERTS, body, 0)

    @pl.when(i == 0)
    def _():
        for_experts(lambda e: _start_all(seg(i, buf, e)))

    @pl.when(i < last)
    def _():
        for_experts(lambda e: _start_all(seg(i + 1, 1 - buf, e)))

    cps.wait()
    cpg.wait()
    for_experts(lambda e: _wait_all(seg(i, buf, e)))
    ybuf_b = ybuf.at[buf]

    def row(j):
        return gate_s[j] * ybuf_b[pl.ds(pl.multiple_of(slot_s[j], SUB), SUB), :]

    def chunk(ci, carry):
        def tok(tb, carry2):
            for u in range(4):
                t = tb * 4 + u
                j = (ci * COMBINE_CHUNK + t) * TOP_K
                acc = row(j)
                for k in range(1, TOP_K):
                    acc = acc + row(j + k)
                mbuf[t] = acc
            return carry2

        lax.fori_loop(0, COMBINE_CHUNK // 4, tok, 0)
        r0 = pl.multiple_of(ci * COMBINE_CHUNK, COMBINE_CHUNK)
        m = mbuf[...].reshape(COMBINE_CHUNK, D)
        o_ref[pl.ds(r0, COMBINE_CHUNK), :] = (x1_ref[pl.ds(r0, COMBINE_CHUNK), :]
                                              + gate_ref[...] * _rms(m, g_ref[...]))
        return carry

    lax.fori_loop(0, MOE_TILE // COMBINE_CHUNK, chunk, 0)


def _combine(ys, slots, gates, nseg, pos, off, x1, gate, g, tiles_per_batch):
    n_tok = x1.shape[0]
    nt = n_tok // MOE_TILE
    grid_spec = pltpu.PrefetchScalarGridSpec(
        num_scalar_prefetch=3,
        grid=(nt,),
        in_specs=[pl.BlockSpec(memory_space=pl.ANY),
                  pl.BlockSpec(memory_space=pl.ANY),
                  pl.BlockSpec(memory_space=pl.ANY),
                  pl.BlockSpec((MOE_TILE, D), lambda i, *_: (i, 0)),
                  pl.BlockSpec((None, 1, D), lambda i, *_: (i // tiles_per_batch, 0, 0)),
                  pl.BlockSpec((1, D), lambda i, *_: (0, 0))],
        out_specs=pl.BlockSpec((MOE_TILE, D), lambda i, *_: (i, 0)),
        scratch_shapes=[pltpu.SMEM((TOP_K * MOE_TILE,), jnp.int32),
                        pltpu.SMEM((TOP_K * MOE_TILE,), f32),
                        pltpu.VMEM((2, TOP_K * MOE_TILE * SUB, LANE), f32),
                        pltpu.VMEM((COMBINE_CHUNK, SUB, LANE), f32),
                        pltpu.SemaphoreType.DMA((2,)),
                        pltpu.SemaphoreType.DMA((2,))],
    )
    return pl.pallas_call(
        _combine_kernel,
        grid_spec=grid_spec,
        out_shape=jax.ShapeDtypeStruct((n_tok, D), f32),
        compiler_params=_params("arbitrary"),
        name="moe_combine",
    )(nseg, pos, off, slots, gates, ys, x1, gate, g)


def _moe_tables(top_i):
    n_tok = top_i.shape[0]
    nt = n_tok // MOE_TILE
    oh4 = (top_i[:, :, None] == jnp.arange(N_EXPERTS)[None, None, :]).astype(f32)
    oh = oh4.sum(axis=1).reshape(nt, MOE_TILE, N_EXPERTS)
    tri = (jnp.arange(MOE_TILE)[:, None] > jnp.arange(MOE_TILE)[None, :]).astype(bf16)
    rank = jnp.einsum('ab,nbe->nae', tri, oh.astype(bf16), preferred_element_type=f32)
    nseg = oh.sum(axis=1).astype(jnp.int32)
    counts = nseg.sum(axis=0)
    pcounts = (counts + EXPERT_BLOCK - 1) // EXPERT_BLOCK * EXPERT_BLOCK
    pend = jnp.cumsum(pcounts)
    pstart = pend - pcounts
    pos = pstart[None, :] + jnp.cumsum(nseg, axis=0) - nseg
    off = jnp.cumsum(nseg, axis=1) - nseg
    base = (off.astype(f32)[:, None, :] + rank).reshape(n_tok, 1, N_EXPERTS)
    slots = (jnp.sum(oh4 * base, axis=-1).astype(jnp.int32) * SUB).reshape(-1)
    n_blocks = -(-(n_tok * TOP_K) // EXPERT_BLOCK) + N_EXPERTS
    blk_start = jnp.arange(n_blocks, dtype=jnp.int32) * EXPERT_BLOCK
    blk_e = jnp.minimum(jnp.sum(blk_start[:, None] >= pend[None, :], axis=1), N_EXPERTS - 1).astype(jnp.int32)
    blk_valid = (blk_start < pend[-1]).astype(jnp.int32)
    blk_first = jnp.concatenate([jnp.ones((1,), jnp.int32), (blk_e[1:] != blk_e[:-1]).astype(jnp.int32)])
    tabs = dict(nseg=nseg.reshape(-1), pos=pos.reshape(-1).astype(jnp.int32), off=off.reshape(-1),
                padn=(pcounts - counts).astype(jnp.int32),
                padpos=jnp.concatenate([pstart + counts, pend[-1:]]).astype(jnp.int32),
                slots=slots, blk_e=blk_e, blk_valid=blk_valid, blk_first=blk_first)
    return tabs, n_blocks * EXPERT_BLOCK


def _moe_residual(h2t, top_i, top_g, x1, gate, g, w_gu, b_gu, w_down, b_down, layer, tiles_per_batch):
    t, n_rows = _moe_tables(top_i)
    xs = _dispatch(h2t, t["slots"], t["nseg"], t["pos"], t["off"], t["padn"], t["padpos"], n_rows)
    ys = _experts(xs.reshape(n_rows, SUB, LANE), t["blk_e"], t["blk_valid"], t["blk_first"],
                  w_gu, b_gu, w_down, b_down, layer)
    return _combine(ys.reshape(n_rows * SUB, LANE), t["slots"], top_g.reshape(-1), t["nseg"], t["pos"], t["off"], x1, gate, g,
                    tiles_per_batch)


def kernel(x, c, ctx, c_ctx, w_ada, b_ada, norm_g, w_qkv, w_o, rpb, w_pw1, b_pw1, w_dw, b_dw, cn_g, cn_b,
           w_pw2, b_pw2, w_router, b_router, w_gu, b_gu, w_down, b_down):
    bsz, seq, _ = x.shape
    tm = min(512, seq)
    cvec = jnp.zeros((8, D), f32).at[:bsz].set(c).at[bsz].set(c_ctx)
    mod = _ada(cvec, w_ada, b_ada)

    def lat(i, k):
        return mod[i, :bsz, k * D:(k + 1) * D][:, None, :]

    def cx(i, k):
        return jnp.broadcast_to(mod[i, bsz, k * D:(k + 1) * D][None, None, :], (bsz, 1, D))

    row = lambda v: v.reshape(1, -1)
    zero_b = jnp.zeros((1, 3 * D), f32)

    wq = w_qkv[0].astype(bf16)
    qkv = _prenorm_matmul(x, row(norm_g[0, 0]), lat(0, 0), lat(0, 1), wq, zero_b,
                          mode="qkv", tm=tm, out_dtype=bf16)
    kv_ctx = _prenorm_matmul(ctx, row(norm_g[0, 0]), cx(0, 0), cx(0, 1), wq, zero_b,
                             mode="qkv", tm=ctx.shape[1], out_dtype=bf16)
    attn = _na_attention(qkv, kv_ctx, _na_bias_table(rpb[0]))
    epi = (x, lat(0, 2), row(norm_g[0, 1]), row(norm_g[0, 2]), lat(0, 3), lat(0, 4),
           w_router[0].astype(bf16), row(b_router[0]))
    x1, h2, ti, tg = _attn_out(attn, w_o[0].astype(bf16), epi, tm=tm)
    x = _moe_residual(h2.reshape(-1, SUB, LANE), ti.reshape(-1, TOP_K), tg.reshape(-1, TOP_K),
                      x1.reshape(-1, D), lat(0, 5), row(norm_g[0, 3]),
                      w_gu, b_gu, w_down, b_down, 0, seq // MOE_TILE).reshape(bsz, seq, D)

    u = _prenorm_matmul(x, row(norm_g[1, 0]), lat(1, 0), lat(1, 1), w_pw1[0].astype(bf16), row(b_pw1[0]),
                        mode="glu", tm=tm, out_dtype=f32)
    epi = (x, lat(1, 2), row(norm_g[1, 1]), row(norm_g[1, 2]), lat(1, 3), lat(1, 4),
           w_router[1].astype(bf16), row(b_router[1]))
    x1, h2, ti, tg = _conv_tail(u, w_dw[0], row(b_dw[0]), row(cn_g[0]), row(cn_b[0]),
                                w_pw2[0].astype(bf16), row(b_pw2[0]), epi, tm=min(256, seq))
    return _moe_residual(h2.reshape(-1, SUB, LANE), ti.reshape(-1, TOP_K), tg.reshape(-1, TOP_K),
                         x1.reshape(-1, D), lat(1, 5), row(norm_g[1, 3]),
                         w_gu, b_gu, w_down, b_down, 1, seq // MOE_TILE).reshape(bsz, seq, D)
```

```python
import functools

import jax
import jax.numpy as jnp
from jax import lax
from jax.experimental import pallas as pl
from jax.experimental.pallas import tpu as pltpu

D = 1024
GRID_W = 64
N_HEADS = 16
HEAD_DIM = 64
HEAD_PAIRS = N_HEADS // 2
WIN_ROWS = 8
WIN_COLS = 16
CONV_W = 31
CONV_HALO = 16
N_EXPERTS = 32
TOP_K = 4
SWIGLU_LIMIT = 7.0
SWIGLU_ALPHA = 1.702
EXPERT_BLOCK = 512
EPS = 1e-6
NEG = -1e30
NA_CHUNK_ROWS = 64
NA_HALO_ROWS = WIN_ROWS // 2
LOG2E = 1.4426950408889634
Q_SCALE = HEAD_DIM ** -0.5 * LOG2E
LANE = 128
SUB = 8
MOE_TILE = 1024
COMBINE_CHUNK = 256
ROW_UNROLL = 8
MOE_TILE_BITS = MOE_TILE.bit_length() - 1
PAD_BITS = EXPERT_BLOCK.bit_length() - 2
V7X_VMEM_LIMIT = 56 * 1024 * 1024

f32 = jnp.float32
bf16 = jnp.bfloat16


def _params(*sem):
    return pltpu.CompilerParams(dimension_semantics=sem, vmem_limit_bytes=V7X_VMEM_LIMIT)


def _rms(v, g):
    return v * lax.rsqrt(jnp.mean(v * v, axis=-1, keepdims=True) + EPS) * g


def _ada_kernel(c_ref, w_ref, b_ref, o_ref):
    c = c_ref[...]
    s = c * jax.nn.sigmoid(c)
    o_ref[...] = jnp.dot(s, w_ref[...], preferred_element_type=f32,
                         precision=lax.Precision.HIGHEST) + b_ref[...]


def _ada(cvec, w_ada, b_ada):
    depth, _, n = w_ada.shape
    tn = 1536
    return pl.pallas_call(
        _ada_kernel,
        grid=(depth, n // tn),
        in_specs=[pl.BlockSpec((8, D), lambda i, j: (0, 0)),
                  pl.BlockSpec((None, D, tn), lambda i, j: (i, 0, j)),
                  pl.BlockSpec((None, 1, tn), lambda i, j: (i, 0, j))],
        out_specs=pl.BlockSpec((None, 8, tn), lambda i, j: (i, 0, j)),
        out_shape=jax.ShapeDtypeStruct((depth, 8, n), f32),
        compiler_params=_params("arbitrary", "arbitrary"),
        name="ada",
    )(cvec, w_ada, b_ada.reshape(depth, 1, n))


def _prenorm_matmul_kernel(x_ref, g_ref, sh_ref, sc_ref, w_ref, b_ref, o_ref, *, mode, tn):
    h = _rms(x_ref[...], g_ref[...]) * (1.0 + sc_ref[...]) + sh_ref[...]
    hb = h.astype(bf16)
    n_out = o_ref.shape[-1]
    for j in range(n_out // tn):
        if mode == "glu":
            a = jnp.dot(hb, w_ref[:, j * tn:(j + 1) * tn], preferred_element_type=f32) + b_ref[:, j * tn:(j + 1) * tn]
            g = (jnp.dot(hb, w_ref[:, n_out + j * tn:n_out + (j + 1) * tn], preferred_element_type=f32)
                 + b_ref[:, n_out + j * tn:n_out + (j + 1) * tn])
            o_ref[:, j * tn:(j + 1) * tn] = (a * jax.nn.sigmoid(g)).astype(o_ref.dtype)
        else:
            a = jnp.dot(hb, w_ref[:, j * tn:(j + 1) * tn], preferred_element_type=f32)
            if mode == "qkv" and (j + 1) * tn <= D:
                a = a * Q_SCALE
            o_ref[:, j * tn:(j + 1) * tn] = a.astype(o_ref.dtype)


def _prenorm_matmul(x, g, sh, sc, w, b, *, mode, tm, out_dtype):
    bsz, seq, _ = x.shape
    n = w.shape[1]
    n_out = n // 2 if mode == "glu" else n
    tn = 512
    kern = functools.partial(_prenorm_matmul_kernel, mode=mode, tn=tn)
    return pl.pallas_call(
        kern,
        grid=(bsz, seq // tm),
        in_specs=[pl.BlockSpec((None, tm, D), lambda b_, t: (b_, t, 0)),
                  pl.BlockSpec((1, D), lambda b_, t: (0, 0)),
                  pl.BlockSpec((None, 1, D), lambda b_, t: (b_, 0, 0)),
                  pl.BlockSpec((None, 1, D), lambda b_, t: (b_, 0, 0)),
                  pl.BlockSpec((D, n), lambda b_, t: (0, 0)),
                  pl.BlockSpec((1, n), lambda b_, t: (0, 0))],
        out_specs=pl.BlockSpec((None, tm, n_out), lambda b_, t: (b_, t, 0)),
        out_shape=jax.ShapeDtypeStruct((bsz, seq, n_out), out_dtype),
        compiler_params=_params("parallel", "parallel"),
        name="prenorm_matmul_" + mode,
    )(x, g, sh, sc, w, b)


def _na_kernel(q_ref, kp_ref, kc_ref, kn_ref, vp_ref, vc_ref, vn_ref, kx_ref, vx_ref, bias_ref, o_ref,
               k_buf, v_buf, vx_buf, *, rows, chunk_rows):
    c = pl.program_id(2)
    blk = chunk_rows * GRID_W
    halo = NA_HALO_ROWS * GRID_W
    hp2 = 2 * HEAD_DIM
    k_buf[0:halo] = kp_ref[...]
    k_buf[halo:halo + blk] = kc_ref[...]
    k_buf[halo + blk:] = kn_ref[...]
    v_buf[0:halo, 0:hp2] = vp_ref[...]
    v_buf[halo:halo + blk, 0:hp2] = vc_ref[...]
    v_buf[halo + blk:, 0:hp2] = vn_ref[...]
    v_buf[:, hp2:] = jnp.ones((blk + 2 * halo, hp2), bf16)
    vx_buf[:, 0:hp2] = vx_ref[...]
    vx_buf[:, hp2:] = jnp.ones((vx_buf.shape[0], hp2), bf16)
    lane = lax.broadcasted_iota(jnp.int32, (GRID_W, hp2), 1)
    first_head = lane < HEAD_DIM
    nt = (((1,), (1,)), ((), ()))
    tn = (((0,), (0,)), ((), ()))
    win = WIN_ROWS * GRID_W

    for i in range(chunk_rows):
        r = c * chunk_rows + i
        rs = jnp.clip(r - WIN_ROWS // 2, 0, rows - WIN_ROWS)
        delta = r - rs
        start = pl.multiple_of((rs - c * chunk_rows + NA_HALO_ROWS) * GRID_W, GRID_W)
        q = q_ref[i * GRID_W:(i + 1) * GRID_W, :]
        zero = jnp.zeros_like(q)
        qbd = jnp.concatenate([jnp.where(first_head, q, zero), jnp.where(first_head, zero, q)], axis=0)
        s_lat = lax.dot_general(k_buf[pl.ds(start, win), :], qbd, nt, preferred_element_type=f32) + bias_ref[delta]
        s_ctx = lax.dot_general(kx_ref[...], qbd, nt, preferred_element_type=f32)
        m = jnp.maximum(jnp.max(s_lat, axis=0, keepdims=True), jnp.max(s_ctx, axis=0, keepdims=True))
        p_lat = jnp.exp2(s_lat - m).astype(bf16)
        p_ctx = jnp.exp2(s_ctx - m).astype(bf16)
        o = (lax.dot_general(p_lat, v_buf[pl.ds(start, win), :], tn, preferred_element_type=f32)
             + lax.dot_general(p_ctx, vx_buf[...], tn, preferred_element_type=f32))
        o = o[:, 0:hp2] * (1.0 / o[:, hp2:hp2 + 1])
        o_ref[i * GRID_W:(i + 1) * GRID_W, :] = jnp.where(
            first_head, o[0:GRID_W], o[GRID_W:2 * GRID_W]).astype(o_ref.dtype)


def _na_bias_table(rpb):
    qc = jnp.arange(GRID_W)
    cs = jnp.clip(qc - WIN_COLS // 2, 0, GRID_W - WIN_COLS)
    kc = jnp.arange(GRID_W)
    inside = (kc[None, :] >= cs[:, None]) & (kc[None, :] < cs[:, None] + WIN_COLS)
    col_off = kc[None, :] - qc[:, None] + (WIN_COLS - 1)
    col_sel = ((col_off[:, :, None] == jnp.arange(2 * WIN_COLS - 1)) & inside[:, :, None]).astype(f32)
    t = jnp.einsum('hrc,qkc->hrqk', rpb.astype(f32), col_sel, precision=lax.Precision.HIGHEST)
    t = jnp.where(inside[None, None], t * LOG2E, NEG)
    t = jnp.stack([t[:, WIN_ROWS - 1 - d:2 * WIN_ROWS - 1 - d] for d in range(WIN_ROWS)], axis=1)
    t = jnp.transpose(t, (1, 2, 4, 0, 3))
    t = t.reshape(WIN_ROWS, WIN_ROWS * GRID_W, HEAD_PAIRS, 2 * GRID_W)
    return jnp.transpose(t, (2, 0, 1, 3))


def _na_attention(qkv, kv_ctx, bias_t):
    bsz, seq, _ = qkv.shape
    ctx_len = kv_ctx.shape[1]
    rows = seq // GRID_W
    chunk_rows = min(NA_CHUNK_ROWS, rows)
    blk = chunk_rows * GRID_W
    halo = NA_HALO_ROWS * GRID_W
    n_chunks = seq // blk
    per = blk // halo
    n_halo = seq // halo
    hp2 = 2 * HEAD_DIM
    kblk = D // hp2
    vblk = 2 * D // hp2

    def cur(col0):
        return pl.BlockSpec((None, blk, hp2), lambda b, h, c: (b, c, col0 + h))

    def before(col0):
        return pl.BlockSpec((None, halo, hp2), lambda b, h, c: (b, jnp.maximum(c * per - 1, 0), col0 + h))

    def after(col0):
        return pl.BlockSpec((None, halo, hp2), lambda b, h, c: (b, jnp.minimum((c + 1) * per, n_halo - 1), col0 + h))

    return pl.pallas_call(
        functools.partial(_na_kernel, rows=rows, chunk_rows=chunk_rows),
        grid=(bsz, HEAD_PAIRS, n_chunks),
        in_specs=[cur(0),
                  before(kblk), cur(kblk), after(kblk),
                  before(vblk), cur(vblk), after(vblk),
                  pl.BlockSpec((None, ctx_len, hp2), lambda b, h, c: (b, 0, kblk + h)),
                  pl.BlockSpec((None, ctx_len, hp2), lambda b, h, c: (b, 0, vblk + h)),
                  pl.BlockSpec((None, WIN_ROWS, WIN_ROWS * GRID_W, hp2), lambda b, h, c: (h, 0, 0, 0))],
        out_specs=pl.BlockSpec((None, blk, hp2), lambda b, h, c: (b, c, h)),
        out_shape=jax.ShapeDtypeStruct((bsz, seq, D), bf16),
        scratch_shapes=[pltpu.VMEM((blk + 2 * halo, hp2), bf16), pltpu.VMEM((blk + 2 * halo, 2 * hp2), bf16),
                        pltpu.VMEM((ctx_len, 2 * hp2), bf16)],
        compiler_params=_params("parallel", "parallel", "arbitrary"),
        name="na_attention",
    )(qkv, qkv, qkv, qkv, qkv, qkv, qkv, kv_ctx, kv_ctx, bias_t)


def _mixer_epilogue(y, x_ref, gate_ref, g_post_ref, g_pre_ref, sh_ref, sc_ref, wr_ref, br_ref,
                    x1_ref, h2_ref, ti_ref, tg_ref):
    x1 = x_ref[...] + gate_ref[...] * _rms(y, g_post_ref[...])
    x1_ref[...] = x1
    h2 = _rms(x1, g_pre_ref[...]) * (1.0 + sc_ref[...]) + sh_ref[...]
    h2_ref[...] = h2.reshape(h2_ref.shape)
    hb = h2.astype(bf16)
    logits = jnp.dot(hb, wr_ref[...], preferred_element_type=f32) + br_ref[...]
    eidx = lax.broadcasted_iota(jnp.int32, logits.shape, 1).astype(f32)
    work = logits
    vals, idxs = [], []
    for _ in range(TOP_K):
        m = jnp.max(work, axis=-1, keepdims=True)
        first = jnp.min(jnp.where(work == m, eidx, float(N_EXPERTS)), axis=-1, keepdims=True)
        vals.append(m)
        idxs.append(first)
        work = jnp.where(eidx == first, NEG, work)
    top_v = jnp.concatenate(vals, axis=-1)
    e = jnp.exp(top_v - vals[0])
    tg_ref[...] = e / jnp.sum(e, axis=-1, keepdims=True)
    ti_ref[...] = jnp.concatenate(idxs, axis=-1).astype(jnp.int32)


def _attn_out_kernel(a_ref, wo_ref, *rest):
    y = jnp.dot(a_ref[...], wo_ref[...], preferred_element_type=f32)
    _mixer_epilogue(y, *rest)


def _epilogue_specs(tm):
    tok = lambda b, t: (b, t, 0)
    per_b = lambda b, t: (b, 0, 0)
    const = lambda b, t: (0, 0)
    in_specs = [pl.BlockSpec((None, tm, D), tok),
                pl.BlockSpec((None, 1, D), per_b),
                pl.BlockSpec((1, D), const),
                pl.BlockSpec((1, D), const),
                pl.BlockSpec((None, 1, D), per_b),
                pl.BlockSpec((None, 1, D), per_b),
                pl.BlockSpec((D, N_EXPERTS), const),
                pl.BlockSpec((1, N_EXPERTS), const)]
    out_specs = [pl.BlockSpec((None, tm, D), tok),
                 pl.BlockSpec((None, tm, SUB, LANE), lambda b, t: (b, t, 0, 0)),
                 pl.BlockSpec((None, tm, TOP_K), tok),
                 pl.BlockSpec((None, tm, TOP_K), tok)]
    return in_specs, out_specs


def _epilogue_shapes(bsz, seq):
    return [jax.ShapeDtypeStruct((bsz, seq, D), f32),
            jax.ShapeDtypeStruct((bsz, seq, SUB, LANE), f32),
            jax.ShapeDtypeStruct((bsz, seq, TOP_K), jnp.int32),
            jax.ShapeDtypeStruct((bsz, seq, TOP_K), f32)]


def _attn_out(a, wo, epi_args, *, tm):
    bsz, seq, _ = a.shape
    in_specs, out_specs = _epilogue_specs(tm)
    return pl.pallas_call(
        _attn_out_kernel,
        grid=(bsz, seq // tm),
        in_specs=[pl.BlockSpec((None, tm, D), lambda b, t: (b, t, 0)),
                  pl.BlockSpec((D, D), lambda b, t: (0, 0))] + in_specs,
        out_specs=out_specs,
        out_shape=_epilogue_shapes(bsz, seq),
        compiler_params=_params("parallel", "parallel"),
        name="attn_out",
    )(a, wo, *epi_args)


def _conv_kernel(up_ref, uc_ref, un_ref, wdw_ref, bdw_ref, lg_ref, lb_ref, w2_ref, b2_ref, *rest, tm, sub):
    ext, shifted = rest[-2:]
    t = pl.program_id(1)
    nt = pl.num_programs(1)
    ext[0:CONV_HALO] = jnp.where(t > 0, up_ref[...], 0.0)
    ext[CONV_HALO:CONV_HALO + tm] = uc_ref[...]
    ext[CONV_HALO + tm:] = jnp.where(t < nt - 1, un_ref[...], 0.0)
    span = tm + 2 * CONV_HALO - SUB
    for a in range(1, SUB):
        shifted[a - 1, 0:span] = ext[a:a + span]
    base = CONV_HALO - CONV_W // 2
    parts = []
    for rc in range(tm // sub):
        acc = jnp.zeros((sub, D), f32) + bdw_ref[...]
        for k in range(CONV_W):
            a, r0 = (base + k) % SUB, rc * sub + (base + k) // SUB * SUB
            win = ext[r0:r0 + sub, :] if a == 0 else shifted[a - 1, r0:r0 + sub, :]
            acc = acc + win * wdw_ref[k:k + 1, :]
        mu = jnp.mean(acc, axis=-1, keepdims=True)
        d = acc - mu
        var = jnp.mean(d * d, axis=-1, keepdims=True)
        z = d * lax.rsqrt(var + EPS) * lg_ref[...] + lb_ref[...]
        parts.append((z * jax.nn.sigmoid(z)).astype(bf16))
    z = jnp.concatenate(parts, axis=0)
    y = jnp.dot(z, w2_ref[...], preferred_element_type=f32) + b2_ref[...]
    _mixer_epilogue(y, *rest[:-2])


def _conv_tail(u, w_dw, b_dw, ln_g, ln_b, w2, b2, epi_args, *, tm):
    bsz, seq, _ = u.shape
    in_specs, out_specs = _epilogue_specs(tm)
    per = tm // CONV_HALO
    n_halo = seq // CONV_HALO
    const = lambda b, t: (0, 0)
    return pl.pallas_call(
        functools.partial(_conv_kernel, tm=tm, sub=8),
        grid=(bsz, seq // tm),
        in_specs=[pl.BlockSpec((None, CONV_HALO, D), lambda b, t: (b, jnp.maximum(t * per - 1, 0), 0)),
                  pl.BlockSpec((None, tm, D), lambda b, t: (b, t, 0)),
                  pl.BlockSpec((None, CONV_HALO, D), lambda b, t: (b, jnp.minimum((t + 1) * per, n_halo - 1), 0)),
                  pl.BlockSpec((CONV_W, D), const),
                  pl.BlockSpec((1, D), const),
                  pl.BlockSpec((1, D), const),
                  pl.BlockSpec((1, D), const),
                  pl.BlockSpec((D, D), const),
                  pl.BlockSpec((1, D), const)] + in_specs,
        out_specs=out_specs,
        out_shape=_epilogue_shapes(bsz, seq),
        scratch_shapes=[pltpu.VMEM((tm + 2 * CONV_HALO, D), f32),
                        pltpu.VMEM((SUB - 1, tm + 2 * CONV_HALO, D), f32)],
        compiler_params=_params("parallel", "arbitrary"),
        name="conv_tail",
    )(u, u, u, w_dw, b_dw, ln_g, ln_b, w2, b2, *epi_args)


def _row_copies(src, src_off, dst, dst_off, n, sem, max_bit, fixed_src=False):
    out = []
    for b in range(max_bit, -1, -1):
        size = SUB << b
        start = (n >> (b + 1)) << (b + 1)
        s_off = src_off if fixed_src else src_off + start
        cp = pltpu.make_async_copy(src.at[pl.ds(pl.multiple_of(s_off * SUB, SUB), size)],
                                   dst.at[pl.ds(pl.multiple_of((dst_off + start) * SUB, SUB), size)], sem)
        out.append((((n >> b) & 1) == 1, cp))
    return out


def _start_all(copies):
    for cond, cp in copies:
        pl.when(cond)(cp.start)


def _wait_all(copies):
    for cond, cp in copies:
        pl.when(cond)(cp.wait)


def _dispatch_kernel(nseg_ref, pos_ref, off_ref, padn_ref, padpos_ref, slots_hbm, x_ref, xs_hbm,
                     slot_s, stage, zeros, sem_s, sem):
    i = pl.program_id(0)
    last = pl.num_programs(0) - 1
    buf = i % 2
    t4 = TOP_K * MOE_TILE
    cp = pltpu.make_async_copy(slots_hbm.at[pl.ds(i * t4, t4)], slot_s, sem_s)
    cp.start()
    cp.wait()
    stage_b = stage.at[buf]

    def tok(tb, carry):
        for u in range(ROW_UNROLL):
            t = tb * ROW_UNROLL + u
            v = x_ref[t]
            for k in range(TOP_K):
                stage_b[pl.ds(pl.multiple_of(slot_s[t * TOP_K + k], SUB), SUB), :] = v
        return carry

    lax.fori_loop(0, MOE_TILE // ROW_UNROLL, tok, 0)

    def seg(tile, b, e):
        j = tile * N_EXPERTS + e
        return _row_copies(stage.at[b], off_ref[j], xs_hbm, pos_ref[j], nseg_ref[j], sem.at[b], MOE_TILE_BITS)

    def for_experts(fn):
        def body(e, carry):
            fn(e)
            return carry
        lax.fori_loop(0, N_EXPERTS, body, 0)

    for_experts(lambda e: _start_all(seg(i, buf, e)))

    @pl.when(i > 0)
    def _():
        for_experts(lambda e: _wait_all(seg(i - 1, 1 - buf, e)))

    @pl.when(i == last)
    def _():
        for_experts(lambda e: _wait_all(seg(i, buf, e)))
        zeros[...] = jnp.zeros_like(zeros)
        zsem = sem.at[0]

        def pad(e):
            return _row_copies(zeros, 0, xs_hbm, padpos_ref[e], padn_ref[e], zsem, PAD_BITS, fixed_src=True)

        for_experts(lambda e: _start_all(pad(e)))
        for_experts(lambda e: _wait_all(pad(e)))

        tail0 = padpos_ref[N_EXPERTS] * SUB
        zrows = zeros.shape[0]

        def tail(j):
            return pltpu.make_async_copy(zeros, xs_hbm.at[pl.ds(pl.multiple_of(tail0 + j * zrows, SUB), zrows)], zsem)

        def tail_start(j, carry):
            tail(j).start()
            return carry

        def tail_wait(j, carry):
            tail(j).wait()
            return carry

        n_tail = (xs_hbm.shape[0] - tail0) // zrows
        lax.fori_loop(0, n_tail, tail_start, 0)
        lax.fori_loop(0, n_tail, tail_wait, 0)


def _dispatch(h2t, slots, nseg, pos, off, padn, padpos, n_rows):
    n_tok = h2t.shape[0]
    nt = n_tok // MOE_TILE
    grid_spec = pltpu.PrefetchScalarGridSpec(
        num_scalar_prefetch=5,
        grid=(nt,),
        in_specs=[pl.BlockSpec(memory_space=pl.ANY),
                  pl.BlockSpec((MOE_TILE, SUB, LANE), lambda i, *_: (i, 0, 0))],
        out_specs=pl.BlockSpec(memory_space=pl.ANY),
        scratch_shapes=[pltpu.SMEM((TOP_K * MOE_TILE,), jnp.int32),
                        pltpu.VMEM((2, TOP_K * MOE_TILE * SUB, LANE), f32),
                        pltpu.VMEM((SUB << PAD_BITS, LANE), f32),
                        pltpu.SemaphoreType.DMA(()),
                        pltpu.SemaphoreType.DMA((2,))],
    )
    return pl.pallas_call(
        _dispatch_kernel,
        grid_spec=grid_spec,
        out_shape=jax.ShapeDtypeStruct((n_rows * SUB, LANE), f32),
        compiler_params=_params("arbitrary"),
        name="moe_dispatch",
    )(nseg, pos, off, padn, padpos, slots, h2t)


def _experts_kernel(be_ref, bv_ref, bf_ref, xs_ref, wgu_ref, bgu_ref, wd_ref, bd_ref, y_ref, wgu_s, wd_s):
    b = pl.program_id(0)

    @pl.when(bf_ref[b] == 1)
    def _():
        wgu_s[...] = wgu_ref[...].astype(bf16)
        wd_s[...] = wd_ref[...].astype(bf16)

    @pl.when(bv_ref[b] == 1)
    def _():
        x = xs_ref[...].reshape(EXPERT_BLOCK, D).astype(bf16)
        gu = jnp.dot(x, wgu_s[...], preferred_element_type=f32) + bgu_ref[...]
        gate = jnp.minimum(gu[:, :D], SWIGLU_LIMIT)
        up = jnp.clip(gu[:, D:], -SWIGLU_LIMIT, SWIGLU_LIMIT)
        glu = gate * jax.nn.sigmoid(gate * SWIGLU_ALPHA)
        a = ((up + 1.0) * glu).astype(bf16)
        y = jnp.dot(a, wd_s[...], preferred_element_type=f32) + bd_ref[...]
        y_ref[...] = y.reshape(EXPERT_BLOCK, SUB, LANE)

    @pl.when(bv_ref[b] == 0)
    def _():
        y_ref[...] = jnp.zeros_like(y_ref)


def _experts(xs, blk_e, blk_valid, blk_first, w_gu, b_gu, w_down, b_down, layer):
    n_rows = xs.shape[0]
    n_blocks = n_rows // EXPERT_BLOCK
    grid_spec = pltpu.PrefetchScalarGridSpec(
        num_scalar_prefetch=3,
        grid=(n_blocks,),
        in_specs=[pl.BlockSpec((EXPERT_BLOCK, SUB, LANE), lambda b, be, bv, bf: (b, 0, 0)),
                  pl.BlockSpec((None, None, D, 2 * D), lambda b, be, bv, bf: (layer, be[b], 0, 0)),
                  pl.BlockSpec((None, None, 1, 2 * D), lambda b, be, bv, bf: (layer, be[b], 0, 0)),
                  pl.BlockSpec((None, None, D, D), lambda b, be, bv, bf: (layer, be[b], 0, 0)),
                  pl.BlockSpec((None, None, 1, D), lambda b, be, bv, bf: (layer, be[b], 0, 0))],
        out_specs=pl.BlockSpec((EXPERT_BLOCK, SUB, LANE), lambda b, be, bv, bf: (b, 0, 0)),
        scratch_shapes=[pltpu.VMEM((D, 2 * D), bf16), pltpu.VMEM((D, D), bf16)],
    )
    return pl.pallas_call(
        _experts_kernel,
        grid_spec=grid_spec,
        out_shape=jax.ShapeDtypeStruct((n_rows, SUB, LANE), f32),
        compiler_params=_params("arbitrary"),
        name="experts",
    )(blk_e, blk_valid, blk_first, xs, w_gu, b_gu[:, :, None, :], w_down, b_down[:, :, None, :])


def _combine_kernel(nseg_ref, pos_ref, off_ref, slots_hbm, gates_hbm, ys_hbm, x1_ref, gate_ref, g_ref, o_ref,
                    slot_s, gate_s, ybuf, mbuf, sem_s, sem):
    i = pl.program_id(0)
    last = pl.num_programs(0) - 1
    buf = i % 2
    t4 = TOP_K * MOE_TILE
    cps = pltpu.make_async_copy(slots_hbm.at[pl.ds(i * t4, t4)], slot_s, sem_s.at[0])
    cpg = pltpu.make_async_copy(gates_hbm.at[pl.ds(i * t4, t4)], gate_s, sem_s.at[1])
    cps.start()
    cpg.start()

    def seg(tile, b, e):
        j = tile * N_EXPERTS + e
        return _row_copies(ys_hbm, pos_ref[j], ybuf.at[b], off_ref[j], nseg_ref[j], sem.at[b], MOE_TILE_BITS)

    def for_experts(fn):
        def body(e, carry):
            fn(e)
            return carry
        lax.fori_loop(0, N_EXPERTS, body, 0)

    @pl.when(i == 0)
    def _():
        for_experts(lambda e: _start_all(seg(i, buf, e)))

    @pl.when(i < last)
    def _():
        for_experts(lambda e: _start_all(seg(i + 1, 1 - buf, e)))

    cps.wait()
    cpg.wait()
    for_experts(lambda e: _wait_all(seg(i, buf, e)))
    ybuf_b = ybuf.at[buf]

    def row(j):
        return gate_s[j] * ybuf_b[pl.ds(pl.multiple_of(slot_s[j], SUB), SUB), :]

    def chunk(ci, carry):
        def tok(tb, carry2):
            for u in range(ROW_UNROLL):
                t = tb * ROW_UNROLL + u
                j = (ci * COMBINE_CHUNK + t) * TOP_K
                acc = row(j)
                for k in range(1, TOP_K):
                    acc = acc + row(j + k)
                mbuf[t] = acc
            return carry2

        lax.fori_loop(0, COMBINE_CHUNK // ROW_UNROLL, tok, 0)
        r0 = pl.multiple_of(ci * COMBINE_CHUNK, COMBINE_CHUNK)
        m = mbuf[...].reshape(COMBINE_CHUNK, D)
        o_ref[pl.ds(r0, COMBINE_CHUNK), :] = (x1_ref[pl.ds(r0, COMBINE_CHUNK), :]
                                              + gate_ref[...] * _rms(m, g_ref[...]))
        return carry

    lax.fori_loop(0, MOE_TILE // COMBINE_CHUNK, chunk, 0)


def _combine(ys, slots, gates, nseg, pos, off, x1, gate, g, tiles_per_batch):
    n_tok = x1.shape[0]
    nt = n_tok // MOE_TILE
    grid_spec = pltpu.PrefetchScalarGridSpec(
        num_scalar_prefetch=3,
        grid=(nt,),
        in_specs=[pl.BlockSpec(memory_space=pl.ANY),
                  pl.BlockSpec(memory_space=pl.ANY),
                  pl.BlockSpec(memory_space=pl.ANY),
                  pl.BlockSpec((MOE_TILE, D), lambda i, *_: (i, 0)),
                  pl.BlockSpec((None, 1, D), lambda i, *_: (i // tiles_per_batch, 0, 0)),
                  pl.BlockSpec((1, D), lambda i, *_: (0, 0))],
        out_specs=pl.BlockSpec((MOE_TILE, D), lambda i, *_: (i, 0)),
        scratch_shapes=[pltpu.SMEM((TOP_K * MOE_TILE,), jnp.int32),
                        pltpu.SMEM((TOP_K * MOE_TILE,), f32),
                        pltpu.VMEM((2, TOP_K * MOE_TILE * SUB, LANE), f32),
                        pltpu.VMEM((COMBINE_CHUNK, SUB, LANE), f32),
                        pltpu.SemaphoreType.DMA((2,)),
                        pltpu.SemaphoreType.DMA((2,))],
    )
    return pl.pallas_call(
        _combine_kernel,
        grid_spec=grid_spec,
        out_shape=jax.ShapeDtypeStruct((n_tok, D), f32),
        compiler_params=_params("arbitrary"),
        name="moe_combine",
    )(nseg, pos, off, slots, gates, ys, x1, gate, g)


def _moe_tables(top_i):
    n_tok = top_i.shape[0]
    nt = n_tok // MOE_TILE
    oh4 = (top_i[:, :, None] == jnp.arange(N_EXPERTS)[None, None, :]).astype(f32)
    oh = oh4.sum(axis=1).reshape(nt, MOE_TILE, N_EXPERTS)
    tri = (jnp.arange(MOE_TILE)[:, None] > jnp.arange(MOE_TILE)[None, :]).astype(bf16)
    rank = jnp.einsum('ab,nbe->nae', tri, oh.astype(bf16), preferred_element_type=f32)
    nseg = oh.sum(axis=1).astype(jnp.int32)
    counts = nseg.sum(axis=0)
    pcounts = (counts + EXPERT_BLOCK - 1) // EXPERT_BLOCK * EXPERT_BLOCK
    pend = jnp.cumsum(pcounts)
    pstart = pend - pcounts
    pos = pstart[None, :] + jnp.cumsum(nseg, axis=0) - nseg
    off = jnp.cumsum(nseg, axis=1) - nseg
    base = (off.astype(f32)[:, None, :] + rank).reshape(n_tok, 1, N_EXPERTS)
    slots = (jnp.sum(oh4 * base, axis=-1).astype(jnp.int32) * SUB).reshape(-1)
    n_blocks = -(-(n_tok * TOP_K) // EXPERT_BLOCK) + N_EXPERTS
    blk_start = jnp.arange(n_blocks, dtype=jnp.int32) * EXPERT_BLOCK
    blk_e = jnp.minimum(jnp.sum(blk_start[:, None] >= pend[None, :], axis=1), N_EXPERTS - 1).astype(jnp.int32)
    blk_valid = (blk_start < pend[-1]).astype(jnp.int32)
    blk_first = jnp.concatenate([jnp.ones((1,), jnp.int32), (blk_e[1:] != blk_e[:-1]).astype(jnp.int32)])
    tabs = dict(nseg=nseg.reshape(-1), pos=pos.reshape(-1).astype(jnp.int32), off=off.reshape(-1),
                padn=(pcounts - counts).astype(jnp.int32),
                padpos=jnp.concatenate([pstart + counts, pend[-1:]]).astype(jnp.int32),
                slots=slots, blk_e=blk_e, blk_valid=blk_valid, blk_first=blk_first)
    return tabs, n_blocks * EXPERT_BLOCK


def _moe_residual(h2t, top_i, top_g, x1, gate, g, w_gu, b_gu, w_down, b_down, layer, tiles_per_batch):
    t, n_rows = _moe_tables(top_i)
    xs = _dispatch(h2t, t["slots"], t["nseg"], t["pos"], t["off"], t["padn"], t["padpos"], n_rows)
    ys = _experts(xs.reshape(n_rows, SUB, LANE), t["blk_e"], t["blk_valid"], t["blk_first"],
                  w_gu, b_gu, w_down, b_down, layer)
    return _combine(ys.reshape(n_rows * SUB, LANE), t["slots"], top_g.reshape(-1), t["nseg"], t["pos"], t["off"], x1, gate, g,
                    tiles_per_batch)


def kernel(x, c, ctx, c_ctx, w_ada, b_ada, norm_g, w_qkv, w_o, rpb, w_pw1, b_pw1, w_dw, b_dw, cn_g, cn_b,
           w_pw2, b_pw2, w_router, b_router, w_gu, b_gu, w_down, b_down):
    bsz, seq, _ = x.shape
    tm = min(512, seq)
    cvec = jnp.zeros((8, D), f32).at[:bsz].set(c).at[bsz].set(c_ctx)
    mod = _ada(cvec, w_ada, b_ada)

    def lat(i, k):
        return mod[i, :bsz, k * D:(k + 1) * D][:, None, :]

    def cx(i, k):
        return jnp.broadcast_to(mod[i, bsz, k * D:(k + 1) * D][None, None, :], (bsz, 1, D))

    row = lambda v: v.reshape(1, -1)
    zero_b = jnp.zeros((1, 3 * D), f32)

    wq = w_qkv[0].astype(bf16)
    qkv = _prenorm_matmul(x, row(norm_g[0, 0]), lat(0, 0), lat(0, 1), wq, zero_b,
                          mode="qkv", tm=tm, out_dtype=bf16)
    kv_ctx = _prenorm_matmul(ctx, row(norm_g[0, 0]), cx(0, 0), cx(0, 1), wq, zero_b,
                             mode="qkv", tm=ctx.shape[1], out_dtype=bf16)
    attn = _na_attention(qkv, kv_ctx, _na_bias_table(rpb[0]))
    epi = (x, lat(0, 2), row(norm_g[0, 1]), row(norm_g[0, 2]), lat(0, 3), lat(0, 4),
           w_router[0].astype(bf16), row(b_router[0]))
    x1, h2, ti, tg = _attn_out(attn, w_o[0].astype(bf16), epi, tm=tm)
    x = _moe_residual(h2.reshape(-1, SUB, LANE), ti.reshape(-1, TOP_K), tg.reshape(-1, TOP_K),
                      x1.reshape(-1, D), lat(0, 5), row(norm_g[0, 3]),
                      w_gu, b_gu, w_down, b_down, 0, seq // MOE_TILE).reshape(bsz, seq, D)

    u = _prenorm_matmul(x, row(norm_g[1, 0]), lat(1, 0), lat(1, 1), w_pw1[0].astype(bf16), row(b_pw1[0]),
                        mode="glu", tm=tm, out_dtype=f32)
    epi = (x, lat(1, 2), row(norm_g[1, 1]), row(norm_g[1, 2]), lat(1, 3), lat(1, 4),
           w_router[1].astype(bf16), row(b_router[1]))
    x1, h2, ti, tg = _conv_tail(u, w_dw[0], row(b_dw[0]), row(cn_g[0]), row(cn_b[0]),
                                w_pw2[0].astype(bf16), row(b_pw2[0]), epi, tm=min(512, seq))
    return _moe_residual(h2.reshape(-1, SUB, LANE), ti.reshape(-1, TOP_K), tg.reshape(-1, TOP_K),
                         x1.reshape(-1, D), lat(1, 5), row(norm_g[1, 3]),
                         w_gu, b_gu, w_down, b_down, 1, seq // MOE_TILE).reshape(bsz, seq, D)
```

```python
import functools

import jax
import jax.numpy as jnp
from jax import lax
from jax.experimental import pallas as pl
from jax.experimental.pallas import tpu as pltpu

D = 1024
GRID_W = 64
N_HEADS = 16
HEAD_DIM = 64
HEAD_PAIRS = N_HEADS // 2
WIN_ROWS = 8
WIN_COLS = 16
CONV_W = 31
CONV_HALO = 16
N_EXPERTS = 32
TOP_K = 4
SWIGLU_LIMIT = 7.0
SWIGLU_ALPHA = 1.702
EXPERT_BLOCK = 512
EPS = 1e-6
NEG = -1e30
NA_CHUNK_ROWS = 64
NA_HALO_ROWS = WIN_ROWS // 2
LOG2E = 1.4426950408889634
Q_SCALE = HEAD_DIM ** -0.5 * LOG2E
LANE = 128
SUB = 8
MOE_TILE = 1024
COMBINE_CHUNK = 256
ROW_UNROLL = 8
MOE_TILE_BITS = MOE_TILE.bit_length() - 1
PAD_BITS = EXPERT_BLOCK.bit_length() - 2
V7X_VMEM_LIMIT = 56 * 1024 * 1024

f32 = jnp.float32
bf16 = jnp.bfloat16


def _params(*sem):
    return pltpu.CompilerParams(dimension_semantics=sem, vmem_limit_bytes=V7X_VMEM_LIMIT)


def _rms(v, g):
    return v * lax.rsqrt(jnp.mean(v * v, axis=-1, keepdims=True) + EPS) * g


def _ada_kernel(c_ref, w_ref, b_ref, o_ref):
    c = c_ref[...]
    s = c * jax.nn.sigmoid(c)
    o_ref[...] = jnp.dot(s, w_ref[...], preferred_element_type=f32,
                         precision=lax.Precision.HIGHEST) + b_ref[...]


def _ada(cvec, w_ada, b_ada):
    depth, _, n = w_ada.shape
    tn = 1536
    return pl.pallas_call(
        _ada_kernel,
        grid=(depth, n // tn),
        in_specs=[pl.BlockSpec((8, D), lambda i, j: (0, 0)),
                  pl.BlockSpec((None, D, tn), lambda i, j: (i, 0, j)),
                  pl.BlockSpec((None, 1, tn), lambda i, j: (i, 0, j))],
        out_specs=pl.BlockSpec((None, 8, tn), lambda i, j: (i, 0, j)),
        out_shape=jax.ShapeDtypeStruct((depth, 8, n), f32),
        compiler_params=_params("arbitrary", "arbitrary"),
        name="ada",
    )(cvec, w_ada, b_ada.reshape(depth, 1, n))


def _prenorm_matmul_kernel(x_ref, g_ref, sh_ref, sc_ref, w_ref, b_ref, o_ref, *, mode, tn):
    h = _rms(x_ref[...], g_ref[...]) * (1.0 + sc_ref[...]) + sh_ref[...]
    hb = h.astype(bf16)
    n_out = o_ref.shape[-1]
    for j in range(n_out // tn):
        if mode == "glu":
            a = jnp.dot(hb, w_ref[:, j * tn:(j + 1) * tn], preferred_element_type=f32) + b_ref[:, j * tn:(j + 1) * tn]
            g = (jnp.dot(hb, w_ref[:, n_out + j * tn:n_out + (j + 1) * tn], preferred_element_type=f32)
                 + b_ref[:, n_out + j * tn:n_out + (j + 1) * tn])
            o_ref[:, j * tn:(j + 1) * tn] = (a * jax.nn.sigmoid(g)).astype(o_ref.dtype)
        else:
            a = jnp.dot(hb, w_ref[:, j * tn:(j + 1) * tn], preferred_element_type=f32)
            if mode == "qkv" and (j + 1) * tn <= D:
                a = a * Q_SCALE
            o_ref[:, j * tn:(j + 1) * tn] = a.astype(o_ref.dtype)


def _prenorm_matmul(x, g, sh, sc, w, b, *, mode, tm, out_dtype):
    bsz, seq, _ = x.shape
    n = w.shape[1]
    n_out = n // 2 if mode == "glu" else n
    tn = 512
    kern = functools.partial(_prenorm_matmul_kernel, mode=mode, tn=tn)
    return pl.pallas_call(
        kern,
        grid=(bsz, seq // tm),
        in_specs=[pl.BlockSpec((None, tm, D), lambda b_, t: (b_, t, 0)),
                  pl.BlockSpec((1, D), lambda b_, t: (0, 0)),
                  pl.BlockSpec((None, 1, D), lambda b_, t: (b_, 0, 0)),
                  pl.BlockSpec((None, 1, D), lambda b_, t: (b_, 0, 0)),
                  pl.BlockSpec((D, n), lambda b_, t: (0, 0)),
                  pl.BlockSpec((1, n), lambda b_, t: (0, 0))],
        out_specs=pl.BlockSpec((None, tm, n_out), lambda b_, t: (b_, t, 0)),
        out_shape=jax.ShapeDtypeStruct((bsz, seq, n_out), out_dtype),
        compiler_params=_params("parallel", "parallel"),
        name="prenorm_matmul_" + mode,
    )(x, g, sh, sc, w, b)


def _na_kernel(q_ref, kp_ref, kc_ref, kn_ref, vp_ref, vc_ref, vn_ref, kx_ref, vx_ref, bias_ref, o_ref,
               k_buf, v_buf, vx_buf, *, rows, chunk_rows):
    c = pl.program_id(2)
    blk = chunk_rows * GRID_W
    halo = NA_HALO_ROWS * GRID_W
    hp2 = 2 * HEAD_DIM
    k_buf[0:halo] = kp_ref[...]
    k_buf[halo:halo + blk] = kc_ref[...]
    k_buf[halo + blk:] = kn_ref[...]
    v_buf[0:halo, 0:hp2] = vp_ref[...]
    v_buf[halo:halo + blk, 0:hp2] = vc_ref[...]
    v_buf[halo + blk:, 0:hp2] = vn_ref[...]
    v_buf[:, hp2:] = jnp.ones((blk + 2 * halo, hp2), bf16)
    vx_buf[:, 0:hp2] = vx_ref[...]
    vx_buf[:, hp2:] = jnp.ones((vx_buf.shape[0], hp2), bf16)
    lane = lax.broadcasted_iota(jnp.int32, (GRID_W, hp2), 1)
    first_head = lane < HEAD_DIM
    nt = (((1,), (1,)), ((), ()))
    tn = (((0,), (0,)), ((), ()))
    win = WIN_ROWS * GRID_W

    for i in range(chunk_rows):
        r = c * chunk_rows + i
        rs = jnp.clip(r - WIN_ROWS // 2, 0, rows - WIN_ROWS)
        delta = r - rs
        start = pl.multiple_of((rs - c * chunk_rows + NA_HALO_ROWS) * GRID_W, GRID_W)
        q = q_ref[i * GRID_W:(i + 1) * GRID_W, :]
        zero = jnp.zeros_like(q)
        qbd = jnp.concatenate([jnp.where(first_head, q, zero), jnp.where(first_head, zero, q)], axis=0)
        s_lat = lax.dot_general(k_buf[pl.ds(start, win), :], qbd, nt, preferred_element_type=f32) + bias_ref[delta]
        s_ctx = lax.dot_general(kx_ref[...], qbd, nt, preferred_element_type=f32)
        m = jnp.maximum(jnp.max(s_lat, axis=0, keepdims=True), jnp.max(s_ctx, axis=0, keepdims=True))
        p_lat = jnp.exp2(s_lat - m).astype(bf16)
        p_ctx = jnp.exp2(s_ctx - m).astype(bf16)
        o = (lax.dot_general(p_lat, v_buf[pl.ds(start, win), :], tn, preferred_element_type=f32)
             + lax.dot_general(p_ctx, vx_buf[...], tn, preferred_element_type=f32))
        o = o[:, 0:hp2] * (1.0 / o[:, hp2:hp2 + 1])
        o_ref[i * GRID_W:(i + 1) * GRID_W, :] = jnp.where(
            first_head, o[0:GRID_W], o[GRID_W:2 * GRID_W]).astype(o_ref.dtype)


def _na_bias_table(rpb):
    qc = jnp.arange(GRID_W)
    cs = jnp.clip(qc - WIN_COLS // 2, 0, GRID_W - WIN_COLS)
    kc = jnp.arange(GRID_W)
    inside = (kc[None, :] >= cs[:, None]) & (kc[None, :] < cs[:, None] + WIN_COLS)
    col_off = kc[None, :] - qc[:, None] + (WIN_COLS - 1)
    col_sel = ((col_off[:, :, None] == jnp.arange(2 * WIN_COLS - 1)) & inside[:, :, None]).astype(f32)
    t = jnp.einsum('hrc,qkc->hrqk', rpb.astype(f32), col_sel, precision=lax.Precision.HIGHEST)
    t = jnp.where(inside[None, None], t * LOG2E, NEG)
    t = jnp.stack([t[:, WIN_ROWS - 1 - d:2 * WIN_ROWS - 1 - d] for d in range(WIN_ROWS)], axis=1)
    t = jnp.transpose(t, (1, 2, 4, 0, 3))
    t = t.reshape(WIN_ROWS, WIN_ROWS * GRID_W, HEAD_PAIRS, 2 * GRID_W)
    return jnp.transpose(t, (2, 0, 1, 3))


def _na_attention(qkv, kv_ctx, bias_t):
    bsz, seq, _ = qkv.shape
    ctx_len = kv_ctx.shape[1]
    rows = seq // GRID_W
    chunk_rows = min(NA_CHUNK_ROWS, rows)
    blk = chunk_rows * GRID_W
    halo = NA_HALO_ROWS * GRID_W
    n_chunks = seq // blk
    per = blk // halo
    n_halo = seq // halo
    hp2 = 2 * HEAD_DIM
    kblk = D // hp2
    vblk = 2 * D // hp2

    def cur(col0):
        return pl.BlockSpec((None, blk, hp2), lambda b, h, c: (b, c, col0 + h))

    def before(col0):
        return pl.BlockSpec((None, halo, hp2), lambda b, h, c: (b, jnp.maximum(c * per - 1, 0), col0 + h))

    def after(col0):
        return pl.BlockSpec((None, halo, hp2), lambda b, h, c: (b, jnp.minimum((c + 1) * per, n_halo - 1), col0 + h))

    return pl.pallas_call(
        functools.partial(_na_kernel, rows=rows, chunk_rows=chunk_rows),
        grid=(bsz, HEAD_PAIRS, n_chunks),
        in_specs=[cur(0),
                  before(kblk), cur(kblk), after(kblk),
                  before(vblk), cur(vblk), after(vblk),
                  pl.BlockSpec((None, ctx_len, hp2), lambda b, h, c: (b, 0, kblk + h)),
                  pl.BlockSpec((None, ctx_len, hp2), lambda b, h, c: (b, 0, vblk + h)),
                  pl.BlockSpec((None, WIN_ROWS, WIN_ROWS * GRID_W, hp2), lambda b, h, c: (h, 0, 0, 0))],
        out_specs=pl.BlockSpec((None, blk, hp2), lambda b, h, c: (b, c, h)),
        out_shape=jax.ShapeDtypeStruct((bsz, seq, D), bf16),
        scratch_shapes=[pltpu.VMEM((blk + 2 * halo, hp2), bf16), pltpu.VMEM((blk + 2 * halo, 2 * hp2), bf16),
                        pltpu.VMEM((ctx_len, 2 * hp2), bf16)],
        compiler_params=_params("parallel", "parallel", "arbitrary"),
        name="na_attention",
    )(qkv, qkv, qkv, qkv, qkv, qkv, qkv, kv_ctx, kv_ctx, bias_t)


def _mixer_epilogue(y, x_ref, gate_ref, g_post_ref, g_pre_ref, sh_ref, sc_ref, wr_ref, br_ref,
                    x1_ref, h2_ref, ti_ref, tg_ref):
    x1 = x_ref[...] + gate_ref[...] * _rms(y, g_post_ref[...])
    x1_ref[...] = x1
    h2 = _rms(x1, g_pre_ref[...]) * (1.0 + sc_ref[...]) + sh_ref[...]
    h2_ref[...] = h2.reshape(h2_ref.shape)
    hb = h2.astype(bf16)
    logits = jnp.dot(hb, wr_ref[...], preferred_element_type=f32) + br_ref[...]
    eidx = lax.broadcasted_iota(jnp.int32, logits.shape, 1).astype(f32)
    work = logits
    vals, idxs = [], []
    for _ in range(TOP_K):
        m = jnp.max(work, axis=-1, keepdims=True)
        first = jnp.min(jnp.where(work == m, eidx, float(N_EXPERTS)), axis=-1, keepdims=True)
        vals.append(m)
        idxs.append(first)
        work = jnp.where(eidx == first, NEG, work)
    top_v = jnp.concatenate(vals, axis=-1)
    e = jnp.exp(top_v - vals[0])
    tg_ref[...] = e / jnp.sum(e, axis=-1, keepdims=True)
    ti_ref[...] = jnp.concatenate(idxs, axis=-1).astype(jnp.int32)


def _attn_out_kernel(a_ref, wo_ref, *rest):
    y = jnp.dot(a_ref[...], wo_ref[...], preferred_element_type=f32)
    _mixer_epilogue(y, *rest)


def _epilogue_specs(tm):
    tok = lambda b, t: (b, t, 0)
    per_b = lambda b, t: (b, 0, 0)
    const = lambda b, t: (0, 0)
    in_specs = [pl.BlockSpec((None, tm, D), tok),
                pl.BlockSpec((None, 1, D), per_b),
                pl.BlockSpec((1, D), const),
                pl.BlockSpec((1, D), const),
                pl.BlockSpec((None, 1, D), per_b),
                pl.BlockSpec((None, 1, D), per_b),
                pl.BlockSpec((D, N_EXPERTS), const),
                pl.BlockSpec((1, N_EXPERTS), const)]
    out_specs = [pl.BlockSpec((None, tm, D), tok),
                 pl.BlockSpec((None, tm, SUB, LANE), lambda b, t: (b, t, 0, 0)),
                 pl.BlockSpec((None, tm, TOP_K), tok),
                 pl.BlockSpec((None, tm, TOP_K), tok)]
    return in_specs, out_specs


def _epilogue_shapes(bsz, seq):
    return [jax.ShapeDtypeStruct((bsz, seq, D), f32),
            jax.ShapeDtypeStruct((bsz, seq, SUB, LANE), f32),
            jax.ShapeDtypeStruct((bsz, seq, TOP_K), jnp.int32),
            jax.ShapeDtypeStruct((bsz, seq, TOP_K), f32)]


def _attn_out(a, wo, epi_args, *, tm):
    bsz, seq, _ = a.shape
    in_specs, out_specs = _epilogue_specs(tm)
    return pl.pallas_call(
        _attn_out_kernel,
        grid=(bsz, seq // tm),
        in_specs=[pl.BlockSpec((None, tm, D), lambda b, t: (b, t, 0)),
                  pl.BlockSpec((D, D), lambda b, t: (0, 0))] + in_specs,
        out_specs=out_specs,
        out_shape=_epilogue_shapes(bsz, seq),
        compiler_params=_params("parallel", "parallel"),
        name="attn_out",
    )(a, wo, *epi_args)


def _conv_kernel(up_ref, uc_ref, un_ref, wdw_ref, bdw_ref, lg_ref, lb_ref, w2_ref, b2_ref, *rest, tm, sub):
    ext, shifted = rest[-2:]
    t = pl.program_id(1)
    nt = pl.num_programs(1)
    ext[0:CONV_HALO] = jnp.where(t > 0, up_ref[...], 0.0)
    ext[CONV_HALO:CONV_HALO + tm] = uc_ref[...]
    ext[CONV_HALO + tm:] = jnp.where(t < nt - 1, un_ref[...], 0.0)
    span = tm + 2 * CONV_HALO - SUB
    for a in range(1, SUB):
        shifted[a - 1, 0:span] = ext[a:a + span]
    base = CONV_HALO - CONV_W // 2
    parts = []
    for rc in range(tm // sub):
        acc = jnp.zeros((sub, D), f32) + bdw_ref[...]
        for k in range(CONV_W):
            a, r0 = (base + k) % SUB, rc * sub + (base + k) // SUB * SUB
            win = ext[r0:r0 + sub, :] if a == 0 else shifted[a - 1, r0:r0 + sub, :]
            acc = acc + win * wdw_ref[k:k + 1, :]
        mu = jnp.mean(acc, axis=-1, keepdims=True)
        d = acc - mu
        var = jnp.mean(d * d, axis=-1, keepdims=True)
        z = d * lax.rsqrt(var + EPS) * lg_ref[...] + lb_ref[...]
        parts.append((z * jax.nn.sigmoid(z)).astype(bf16))
    z = jnp.concatenate(parts, axis=0)
    y = jnp.dot(z, w2_ref[...], preferred_element_type=f32) + b2_ref[...]
    _mixer_epilogue(y, *rest[:-2])


def _conv_tail(u, w_dw, b_dw, ln_g, ln_b, w2, b2, epi_args, *, tm):
    bsz, seq, _ = u.shape
    in_specs, out_specs = _epilogue_specs(tm)
    per = tm // CONV_HALO
    n_halo = seq // CONV_HALO
    const = lambda b, t: (0, 0)
    return pl.pallas_call(
        functools.partial(_conv_kernel, tm=tm, sub=8),
        grid=(bsz, seq // tm),
        in_specs=[pl.BlockSpec((None, CONV_HALO, D), lambda b, t: (b, jnp.maximum(t * per - 1, 0), 0)),
                  pl.BlockSpec((None, tm, D), lambda b, t: (b, t, 0)),
                  pl.BlockSpec((None, CONV_HALO, D), lambda b, t: (b, jnp.minimum((t + 1) * per, n_halo - 1), 0)),
                  pl.BlockSpec((CONV_W, D), const),
                  pl.BlockSpec((1, D), const),
                  pl.BlockSpec((1, D), const),
                  pl.BlockSpec((1, D), const),
                  pl.BlockSpec((D, D), const),
                  pl.BlockSpec((1, D), const)] + in_specs,
        out_specs=out_specs,
        out_shape=_epilogue_shapes(bsz, seq),
        scratch_shapes=[pltpu.VMEM((tm + 2 * CONV_HALO, D), f32),
                        pltpu.VMEM((SUB - 1, tm + 2 * CONV_HALO, D), f32)],
        compiler_params=_params("parallel", "arbitrary"),
        name="conv_tail",
    )(u, u, u, w_dw, b_dw, ln_g, ln_b, w2, b2, *epi_args)


def _row_copies(src, src_off, dst, dst_off, n, sem, max_bit, fixed_src=False):
    out = []
    for b in range(max_bit, -1, -1):
        size = SUB << b
        start = (n >> (b + 1)) << (b + 1)
        s_off = src_off if fixed_src else src_off + start
        cp = pltpu.make_async_copy(src.at[pl.ds(pl.multiple_of(s_off * SUB, SUB), size)],
                                   dst.at[pl.ds(pl.multiple_of((dst_off + start) * SUB, SUB), size)], sem)
        out.append((((n >> b) & 1) == 1, cp))
    return out


def _start_all(copies):
    for cond, cp in copies:
        pl.when(cond)(cp.start)


def _wait_all(copies):
    for cond, cp in copies:
        pl.when(cond)(cp.wait)


def _segment_copies(lists, tile, src, dst, sem, src_is_tile, action):
    cnt_ref, local_ref, glob_ref = lists
    classes = MOE_TILE_BITS + 1
    for b in range(classes):
        size = SUB << b
        base = (tile * classes + b) * N_EXPERTS

        def body(q, carry, size=size, base=base):
            s_row, d_row = (local_ref, glob_ref) if src_is_tile else (glob_ref, local_ref)
            action(pltpu.make_async_copy(
                src.at[pl.ds(pl.multiple_of(s_row[base + q] * SUB, SUB), size)],
                dst.at[pl.ds(pl.multiple_of(d_row[base + q] * SUB, SUB), size)], sem))
            return carry

        lax.fori_loop(0, cnt_ref[tile * classes + b], body, 0)


def _dispatch_kernel(cnt_ref, local_ref, glob_ref, padn_ref, padpos_ref, slots_hbm, x_ref, xs_hbm,
                     slot_s, stage, zeros, sem_s, sem):
    i = pl.program_id(0)
    last = pl.num_programs(0) - 1
    buf = i % 2
    t4 = TOP_K * MOE_TILE
    cp = pltpu.make_async_copy(slots_hbm.at[pl.ds(i * t4, t4)], slot_s, sem_s)
    cp.start()
    cp.wait()
    stage_b = stage.at[buf]

    def tok(tb, carry):
        for u in range(ROW_UNROLL):
            t = tb * ROW_UNROLL + u
            v = x_ref[t]
            for k in range(TOP_K):
                stage_b[pl.ds(pl.multiple_of(slot_s[t * TOP_K + k], SUB), SUB), :] = v
        return carry

    lax.fori_loop(0, MOE_TILE // ROW_UNROLL, tok, 0)

    lists = (cnt_ref, local_ref, glob_ref)

    def segments(tile, b, action):
        _segment_copies(lists, tile, stage.at[b], xs_hbm, sem.at[b], True, action)

    def for_experts(fn):
        def body(e, carry):
            fn(e)
            return carry
        lax.fori_loop(0, N_EXPERTS, body, 0)

    segments(i, buf, lambda cp: cp.start())

    @pl.when(i > 0)
    def _():
        segments(i - 1, 1 - buf, lambda cp: cp.wait())

    @pl.when(i == last)
    def _():
        segments(i, buf, lambda cp: cp.wait())
        zeros[...] = jnp.zeros_like(zeros)
        zsem = sem.at[0]

        def pad(e):
            return _row_copies(zeros, 0, xs_hbm, padpos_ref[e], padn_ref[e], zsem, PAD_BITS, fixed_src=True)

        for_experts(lambda e: _start_all(pad(e)))
        for_experts(lambda e: _wait_all(pad(e)))

        tail0 = padpos_ref[N_EXPERTS] * SUB
        zrows = zeros.shape[0]

        def tail(j):
            return pltpu.make_async_copy(zeros, xs_hbm.at[pl.ds(pl.multiple_of(tail0 + j * zrows, SUB), zrows)], zsem)

        def tail_start(j, carry):
            tail(j).start()
            return carry

        def tail_wait(j, carry):
            tail(j).wait()
            return carry

        n_tail = (xs_hbm.shape[0] - tail0) // zrows
        lax.fori_loop(0, n_tail, tail_start, 0)
        lax.fori_loop(0, n_tail, tail_wait, 0)


def _dispatch(h2t, slots, seg_lists, padn, padpos, n_rows):
    n_tok = h2t.shape[0]
    nt = n_tok // MOE_TILE
    grid_spec = pltpu.PrefetchScalarGridSpec(
        num_scalar_prefetch=5,
        grid=(nt,),
        in_specs=[pl.BlockSpec(memory_space=pl.ANY),
                  pl.BlockSpec((MOE_TILE, SUB, LANE), lambda i, *_: (i, 0, 0))],
        out_specs=pl.BlockSpec(memory_space=pl.ANY),
        scratch_shapes=[pltpu.SMEM((TOP_K * MOE_TILE,), jnp.int32),
                        pltpu.VMEM((2, TOP_K * MOE_TILE * SUB, LANE), f32),
                        pltpu.VMEM((SUB << PAD_BITS, LANE), f32),
                        pltpu.SemaphoreType.DMA(()),
                        pltpu.SemaphoreType.DMA((2,))],
    )
    return pl.pallas_call(
        _dispatch_kernel,
        grid_spec=grid_spec,
        out_shape=jax.ShapeDtypeStruct((n_rows * SUB, LANE), f32),
        compiler_params=_params("arbitrary"),
        name="moe_dispatch",
    )(*seg_lists, padn, padpos, slots, h2t)


def _experts_kernel(be_ref, bv_ref, bf_ref, xs_ref, wgu_ref, bgu_ref, wd_ref, bd_ref, y_ref, wgu_s, wd_s):
    b = pl.program_id(0)

    @pl.when(bf_ref[b] == 1)
    def _():
        wgu_s[...] = wgu_ref[...].astype(bf16)
        wd_s[...] = wd_ref[...].astype(bf16)

    @pl.when(bv_ref[b] == 1)
    def _():
        x = xs_ref[...].reshape(EXPERT_BLOCK, D).astype(bf16)
        gu = jnp.dot(x, wgu_s[...], preferred_element_type=f32) + bgu_ref[...]
        gate = jnp.minimum(gu[:, :D], SWIGLU_LIMIT)
        up = jnp.clip(gu[:, D:], -SWIGLU_LIMIT, SWIGLU_LIMIT)
        glu = gate * jax.nn.sigmoid(gate * SWIGLU_ALPHA)
        a = ((up + 1.0) * glu).astype(bf16)
        y = jnp.dot(a, wd_s[...], preferred_element_type=f32) + bd_ref[...]
        y_ref[...] = y.reshape(EXPERT_BLOCK, SUB, LANE)

    @pl.when(bv_ref[b] == 0)
    def _():
        y_ref[...] = jnp.zeros_like(y_ref)


def _experts(xs, blk_e, blk_valid, blk_first, w_gu, b_gu, w_down, b_down, layer):
    n_rows = xs.shape[0]
    n_blocks = n_rows // EXPERT_BLOCK
    grid_spec = pltpu.PrefetchScalarGridSpec(
        num_scalar_prefetch=3,
        grid=(n_blocks,),
        in_specs=[pl.BlockSpec((EXPERT_BLOCK, SUB, LANE), lambda b, be, bv, bf: (b, 0, 0)),
                  pl.BlockSpec((None, None, D, 2 * D), lambda b, be, bv, bf: (layer, be[b], 0, 0)),
                  pl.BlockSpec((None, None, 1, 2 * D), lambda b, be, bv, bf: (layer, be[b], 0, 0)),
                  pl.BlockSpec((None, None, D, D), lambda b, be, bv, bf: (layer, be[b], 0, 0)),
                  pl.BlockSpec((None, None, 1, D), lambda b, be, bv, bf: (layer, be[b], 0, 0))],
        out_specs=pl.BlockSpec((EXPERT_BLOCK, SUB, LANE), lambda b, be, bv, bf: (b, 0, 0)),
        scratch_shapes=[pltpu.VMEM((D, 2 * D), bf16), pltpu.VMEM((D, D), bf16)],
    )
    return pl.pallas_call(
        _experts_kernel,
        grid_spec=grid_spec,
        out_shape=jax.ShapeDtypeStruct((n_rows, SUB, LANE), f32),
        compiler_params=_params("arbitrary"),
        name="experts",
    )(blk_e, blk_valid, blk_first, xs, w_gu, b_gu[:, :, None, :], w_down, b_down[:, :, None, :])


def _combine_kernel(cnt_ref, local_ref, glob_ref, slots_hbm, gates_hbm, ys_hbm, x1_ref, gate_ref, g_ref, o_ref,
                    slot_s, gate_s, ybuf, mbuf, sem_s, sem):
    i = pl.program_id(0)
    last = pl.num_programs(0) - 1
    buf = i % 2
    t4 = TOP_K * MOE_TILE
    cps = pltpu.make_async_copy(slots_hbm.at[pl.ds(i * t4, t4)], slot_s, sem_s.at[0])
    cpg = pltpu.make_async_copy(gates_hbm.at[pl.ds(i * t4, t4)], gate_s, sem_s.at[1])
    cps.start()
    cpg.start()

    lists = (cnt_ref, local_ref, glob_ref)

    def segments(tile, b, action):
        _segment_copies(lists, tile, ys_hbm, ybuf.at[b], sem.at[b], False, action)

    @pl.when(i == 0)
    def _():
        segments(i, buf, lambda cp: cp.start())

    @pl.when(i < last)
    def _():
        segments(i + 1, 1 - buf, lambda cp: cp.start())

    cps.wait()
    cpg.wait()
    segments(i, buf, lambda cp: cp.wait())
    ybuf_b = ybuf.at[buf]

    def row(j):
        return gate_s[j] * ybuf_b[pl.ds(pl.multiple_of(slot_s[j], SUB), SUB), :]

    def chunk(ci, carry):
        def tok(tb, carry2):
            for u in range(ROW_UNROLL):
                t = tb * ROW_UNROLL + u
                j = (ci * COMBINE_CHUNK + t) * TOP_K
                acc = row(j)
                for k in range(1, TOP_K):
                    acc = acc + row(j + k)
                mbuf[t] = acc
            return carry2

        lax.fori_loop(0, COMBINE_CHUNK // ROW_UNROLL, tok, 0)
        r0 = pl.multiple_of(ci * COMBINE_CHUNK, COMBINE_CHUNK)
        m = mbuf[...].reshape(COMBINE_CHUNK, D)
        o_ref[pl.ds(r0, COMBINE_CHUNK), :] = (x1_ref[pl.ds(r0, COMBINE_CHUNK), :]
                                              + gate_ref[...] * _rms(m, g_ref[...]))
        return carry

    lax.fori_loop(0, MOE_TILE // COMBINE_CHUNK, chunk, 0)


def _combine(ys, slots, gates, seg_lists, x1, gate, g, tiles_per_batch):
    n_tok = x1.shape[0]
    nt = n_tok // MOE_TILE
    grid_spec = pltpu.PrefetchScalarGridSpec(
        num_scalar_prefetch=3,
        grid=(nt,),
        in_specs=[pl.BlockSpec(memory_space=pl.ANY),
                  pl.BlockSpec(memory_space=pl.ANY),
                  pl.BlockSpec(memory_space=pl.ANY),
                  pl.BlockSpec((MOE_TILE, D), lambda i, *_: (i, 0)),
                  pl.BlockSpec((None, 1, D), lambda i, *_: (i // tiles_per_batch, 0, 0)),
                  pl.BlockSpec((1, D), lambda i, *_: (0, 0))],
        out_specs=pl.BlockSpec((MOE_TILE, D), lambda i, *_: (i, 0)),
        scratch_shapes=[pltpu.SMEM((TOP_K * MOE_TILE,), jnp.int32),
                        pltpu.SMEM((TOP_K * MOE_TILE,), f32),
                        pltpu.VMEM((2, TOP_K * MOE_TILE * SUB, LANE), f32),
                        pltpu.VMEM((COMBINE_CHUNK, SUB, LANE), f32),
                        pltpu.SemaphoreType.DMA((2,)),
                        pltpu.SemaphoreType.DMA((2,))],
    )
    return pl.pallas_call(
        _combine_kernel,
        grid_spec=grid_spec,
        out_shape=jax.ShapeDtypeStruct((n_tok, D), f32),
        compiler_params=_params("arbitrary"),
        name="moe_combine",
    )(*seg_lists, slots, gates, ys, x1, gate, g)


def _moe_tables(top_i):
    n_tok = top_i.shape[0]
    nt = n_tok // MOE_TILE
    oh4 = (top_i[:, :, None] == jnp.arange(N_EXPERTS)[None, None, :]).astype(f32)
    oh = oh4.sum(axis=1).reshape(nt, MOE_TILE, N_EXPERTS)
    tri = (jnp.arange(MOE_TILE)[:, None] > jnp.arange(MOE_TILE)[None, :]).astype(bf16)
    rank = jnp.einsum('ab,nbe->nae', tri, oh.astype(bf16), preferred_element_type=f32)
    nseg = oh.sum(axis=1).astype(jnp.int32)
    counts = nseg.sum(axis=0)
    pcounts = (counts + EXPERT_BLOCK - 1) // EXPERT_BLOCK * EXPERT_BLOCK
    pend = jnp.cumsum(pcounts)
    pstart = pend - pcounts
    pos = pstart[None, :] + jnp.cumsum(nseg, axis=0) - nseg
    off = jnp.cumsum(nseg, axis=1) - nseg
    base = (off.astype(f32)[:, None, :] + rank).reshape(n_tok, 1, N_EXPERTS)
    slots = (jnp.sum(oh4 * base, axis=-1).astype(jnp.int32) * SUB).reshape(-1)
    n_blocks = -(-(n_tok * TOP_K) // EXPERT_BLOCK) + N_EXPERTS
    blk_start = jnp.arange(n_blocks, dtype=jnp.int32) * EXPERT_BLOCK
    blk_e = jnp.minimum(jnp.sum(blk_start[:, None] >= pend[None, :], axis=1), N_EXPERTS - 1).astype(jnp.int32)
    blk_valid = (blk_start < pend[-1]).astype(jnp.int32)
    blk_first = jnp.concatenate([jnp.ones((1,), jnp.int32), (blk_e[1:] != blk_e[:-1]).astype(jnp.int32)])
    bit = jnp.arange(MOE_TILE_BITS + 1, dtype=jnp.int32)[None, :, None]
    n3 = nseg[:, None, :]
    has = (n3 >> bit) & 1
    first = (n3 >> (bit + 1)) << (bit + 1)
    place = jnp.cumsum(has, axis=2) - has
    put = ((place[..., None] == jnp.arange(N_EXPERTS)) & (has[..., None] == 1)).astype(jnp.int32)
    local = jnp.sum(put * (off[:, None, :] + first)[..., None], axis=2)
    glob = jnp.sum(put * (pos[:, None, :].astype(jnp.int32) + first)[..., None], axis=2)
    seg_lists = (has.sum(axis=2).reshape(-1).astype(jnp.int32), local.reshape(-1).astype(jnp.int32),
                 glob.reshape(-1).astype(jnp.int32))
    tabs = dict(seg_lists=seg_lists,
                padn=(pcounts - counts).astype(jnp.int32),
                padpos=jnp.concatenate([pstart + counts, pend[-1:]]).astype(jnp.int32),
                slots=slots, blk_e=blk_e, blk_valid=blk_valid, blk_first=blk_first)
    return tabs, n_blocks * EXPERT_BLOCK


def _moe_residual(h2t, top_i, top_g, x1, gate, g, w_gu, b_gu, w_down, b_down, layer, tiles_per_batch):
    t, n_rows = _moe_tables(top_i)
    xs = _dispatch(h2t, t["slots"], t["seg_lists"], t["padn"], t["padpos"], n_rows)
    ys = _experts(xs.reshape(n_rows, SUB, LANE), t["blk_e"], t["blk_valid"], t["blk_first"],
                  w_gu, b_gu, w_down, b_down, layer)
    return _combine(ys.reshape(n_rows * SUB, LANE), t["slots"], top_g.reshape(-1), t["seg_lists"], x1, gate, g,
                    tiles_per_batch)


def kernel(x, c, ctx, c_ctx, w_ada, b_ada, norm_g, w_qkv, w_o, rpb, w_pw1, b_pw1, w_dw, b_dw, cn_g, cn_b,
           w_pw2, b_pw2, w_router, b_router, w_gu, b_gu, w_down, b_down):
    bsz, seq, _ = x.shape
    tm = min(512, seq)
    cvec = jnp.zeros((8, D), f32).at[:bsz].set(c).at[bsz].set(c_ctx)
    mod = _ada(cvec, w_ada, b_ada)

    def lat(i, k):
        return mod[i, :bsz, k * D:(k + 1) * D][:, None, :]

    def cx(i, k):
        return jnp.broadcast_to(mod[i, bsz, k * D:(k + 1) * D][None, None, :], (bsz, 1, D))

    row = lambda v: v.reshape(1, -1)
    zero_b = jnp.zeros((1, 3 * D), f32)

    wq = w_qkv[0].astype(bf16)
    qkv = _prenorm_matmul(x, row(norm_g[0, 0]), lat(0, 0), lat(0, 1), wq, zero_b,
                          mode="qkv", tm=tm, out_dtype=bf16)
    kv_ctx = _prenorm_matmul(ctx, row(norm_g[0, 0]), cx(0, 0), cx(0, 1), wq, zero_b,
                             mode="qkv", tm=ctx.shape[1], out_dtype=bf16)
    attn = _na_attention(qkv, kv_ctx, _na_bias_table(rpb[0]))
    epi = (x, lat(0, 2), row(norm_g[0, 1]), row(norm_g[0, 2]), lat(0, 3), lat(0, 4),
           w_router[0].astype(bf16), row(b_router[0]))
    x1, h2, ti, tg = _attn_out(attn, w_o[0].astype(bf16), epi, tm=tm)
    x = _moe_residual(h2.reshape(-1, SUB, LANE), ti.reshape(-1, TOP_K), tg.reshape(-1, TOP_K),
                      x1.reshape(-1, D), lat(0, 5), row(norm_g[0, 3]),
                      w_gu, b_gu, w_down, b_down, 0, seq // MOE_TILE).reshape(bsz, seq, D)

    u = _prenorm_matmul(x, row(norm_g[1, 0]), lat(1, 0), lat(1, 1), w_pw1[0].astype(bf16), row(b_pw1[0]),
                        mode="glu", tm=tm, out_dtype=f32)
    epi = (x, lat(1, 2), row(norm_g[1, 1]), row(norm_g[1, 2]), lat(1, 3), lat(1, 4),
           w_router[1].astype(bf16), row(b_router[1]))
    x1, h2, ti, tg = _conv_tail(u, w_dw[0], row(b_dw[0]), row(cn_g[0]), row(cn_b[0]),
                                w_pw2[0].astype(bf16), row(b_pw2[0]), epi, tm=min(512, seq))
    return _moe_residual(h2.reshape(-1, SUB, LANE), ti.reshape(-1, TOP_K), tg.reshape(-1, TOP_K),
                         x1.reshape(-1, D), lat(1, 5), row(norm_g[1, 3]),
                         w_gu, b_gu, w_down, b_down, 1, seq // MOE_TILE).reshape(bsz, seq, D)
```

```python
import functools

import jax
import jax.numpy as jnp
from jax import lax
from jax.experimental import pallas as pl
from jax.experimental.pallas import tpu as pltpu

D = 1024
GRID_W = 64
N_HEADS = 16
HEAD_DIM = 64
HEAD_PAIRS = N_HEADS // 2
WIN_ROWS = 8
WIN_COLS = 16
CONV_W = 31
CONV_HALO = 16
N_EXPERTS = 32
TOP_K = 4
SWIGLU_LIMIT = 7.0
SWIGLU_ALPHA = 1.702
EXPERT_BLOCK = 512
EPS = 1e-6
NEG = -1e30
NA_CHUNK_ROWS = 64
NA_HALO_ROWS = WIN_ROWS // 2
LOG2E = 1.4426950408889634
Q_SCALE = HEAD_DIM ** -0.5 * LOG2E
LANE = 128
SUB = 8
MOE_TILE = 1024
COMBINE_CHUNK = 256
ROW_UNROLL = 8
MOE_TILE_BITS = MOE_TILE.bit_length() - 1
PAD_BITS = EXPERT_BLOCK.bit_length() - 2
V7X_VMEM_LIMIT = 56 * 1024 * 1024

f32 = jnp.float32
bf16 = jnp.bfloat16


def _params(*sem):
    return pltpu.CompilerParams(dimension_semantics=sem, vmem_limit_bytes=V7X_VMEM_LIMIT)


def _rms(v, g):
    return v * lax.rsqrt(jnp.mean(v * v, axis=-1, keepdims=True) + EPS) * g


def _ada_kernel(c_ref, w_ref, b_ref, o_ref):
    c = c_ref[...]
    s = c * jax.nn.sigmoid(c)
    o_ref[...] = jnp.dot(s, w_ref[...], preferred_element_type=f32,
                         precision=lax.Precision.HIGHEST) + b_ref[...]


def _ada(cvec, w_ada, b_ada):
    depth, _, n = w_ada.shape
    tn = 1536
    return pl.pallas_call(
        _ada_kernel,
        grid=(depth, n // tn),
        in_specs=[pl.BlockSpec((8, D), lambda i, j: (0, 0)),
                  pl.BlockSpec((None, D, tn), lambda i, j: (i, 0, j)),
                  pl.BlockSpec((None, 1, tn), lambda i, j: (i, 0, j))],
        out_specs=pl.BlockSpec((None, 8, tn), lambda i, j: (i, 0, j)),
        out_shape=jax.ShapeDtypeStruct((depth, 8, n), f32),
        compiler_params=_params("arbitrary", "arbitrary"),
        name="ada",
    )(cvec, w_ada, b_ada.reshape(depth, 1, n))


def _prenorm_matmul_kernel(x_ref, g_ref, sh_ref, sc_ref, w_ref, b_ref, o_ref, *, mode, tn):
    h = _rms(x_ref[...], g_ref[...]) * (1.0 + sc_ref[...]) + sh_ref[...]
    hb = h.astype(bf16)
    n_out = o_ref.shape[-1]
    for j in range(n_out // tn):
        if mode == "glu":
            a = jnp.dot(hb, w_ref[:, j * tn:(j + 1) * tn], preferred_element_type=f32) + b_ref[:, j * tn:(j + 1) * tn]
            g = (jnp.dot(hb, w_ref[:, n_out + j * tn:n_out + (j + 1) * tn], preferred_element_type=f32)
                 + b_ref[:, n_out + j * tn:n_out + (j + 1) * tn])
            o_ref[:, j * tn:(j + 1) * tn] = (a * jax.nn.sigmoid(g)).astype(o_ref.dtype)
        else:
            a = jnp.dot(hb, w_ref[:, j * tn:(j + 1) * tn], preferred_element_type=f32)
            if mode == "qkv" and (j + 1) * tn <= D:
                a = a * Q_SCALE
            o_ref[:, j * tn:(j + 1) * tn] = a.astype(o_ref.dtype)


def _prenorm_matmul(x, g, sh, sc, w, b, *, mode, tm, out_dtype):
    bsz, seq, _ = x.shape
    n = w.shape[1]
    n_out = n // 2 if mode == "glu" else n
    tn = 512
    kern = functools.partial(_prenorm_matmul_kernel, mode=mode, tn=tn)
    return pl.pallas_call(
        kern,
        grid=(bsz, seq // tm),
        in_specs=[pl.BlockSpec((None, tm, D), lambda b_, t: (b_, t, 0)),
                  pl.BlockSpec((1, D), lambda b_, t: (0, 0)),
                  pl.BlockSpec((None, 1, D), lambda b_, t: (b_, 0, 0)),
                  pl.BlockSpec((None, 1, D), lambda b_, t: (b_, 0, 0)),
                  pl.BlockSpec((D, n), lambda b_, t: (0, 0)),
                  pl.BlockSpec((1, n), lambda b_, t: (0, 0))],
        out_specs=pl.BlockSpec((None, tm, n_out), lambda b_, t: (b_, t, 0)),
        out_shape=jax.ShapeDtypeStruct((bsz, seq, n_out), out_dtype),
        compiler_params=_params("parallel", "parallel"),
        name="prenorm_matmul_" + mode,
    )(x, g, sh, sc, w, b)


def _na_kernel(q_ref, kp_ref, kc_ref, kn_ref, vp_ref, vc_ref, vn_ref, kx_ref, vx_ref, bias_ref, o_ref,
               k_buf, v_buf, vx_buf, *, rows, chunk_rows):
    c = pl.program_id(2)
    blk = chunk_rows * GRID_W
    halo = NA_HALO_ROWS * GRID_W
    hp2 = 2 * HEAD_DIM
    k_buf[0:halo] = kp_ref[...]
    k_buf[halo:halo + blk] = kc_ref[...]
    k_buf[halo + blk:] = kn_ref[...]
    v_buf[0:halo, 0:hp2] = vp_ref[...]
    v_buf[halo:halo + blk, 0:hp2] = vc_ref[...]
    v_buf[halo + blk:, 0:hp2] = vn_ref[...]
    v_buf[:, hp2:] = jnp.ones((blk + 2 * halo, hp2), bf16)
    vx_buf[:, 0:hp2] = vx_ref[...]
    vx_buf[:, hp2:] = jnp.ones((vx_buf.shape[0], hp2), bf16)
    lane = lax.broadcasted_iota(jnp.int32, (GRID_W, hp2), 1)
    first_head = lane < HEAD_DIM
    nt = (((1,), (1,)), ((), ()))
    tn = (((0,), (0,)), ((), ()))
    win = WIN_ROWS * GRID_W

    for i in range(chunk_rows):
        r = c * chunk_rows + i
        rs = jnp.clip(r - WIN_ROWS // 2, 0, rows - WIN_ROWS)
        delta = r - rs
        start = pl.multiple_of((rs - c * chunk_rows + NA_HALO_ROWS) * GRID_W, GRID_W)
        q = q_ref[i * GRID_W:(i + 1) * GRID_W, :]
        zero = jnp.zeros_like(q)
        qbd = jnp.concatenate([jnp.where(first_head, q, zero), jnp.where(first_head, zero, q)], axis=0)
        s_lat = lax.dot_general(k_buf[pl.ds(start, win), :], qbd, nt, preferred_element_type=f32) + bias_ref[delta]
        s_ctx = lax.dot_general(kx_ref[...], qbd, nt, preferred_element_type=f32)
        m = jnp.maximum(jnp.max(s_lat, axis=0, keepdims=True), jnp.max(s_ctx, axis=0, keepdims=True))
        p_lat = jnp.exp2(s_lat - m).astype(bf16)
        p_ctx = jnp.exp2(s_ctx - m).astype(bf16)
        o = (lax.dot_general(p_lat, v_buf[pl.ds(start, win), :], tn, preferred_element_type=f32)
             + lax.dot_general(p_ctx, vx_buf[...], tn, preferred_element_type=f32))
        o = o[:, 0:hp2] * (1.0 / o[:, hp2:hp2 + 1])
        o_ref[i * GRID_W:(i + 1) * GRID_W, :] = jnp.where(
            first_head, o[0:GRID_W], o[GRID_W:2 * GRID_W]).astype(o_ref.dtype)


def _na_bias_table(rpb):
    qc = jnp.arange(GRID_W)
    cs = jnp.clip(qc - WIN_COLS // 2, 0, GRID_W - WIN_COLS)
    kc = jnp.arange(GRID_W)
    inside = (kc[None, :] >= cs[:, None]) & (kc[None, :] < cs[:, None] + WIN_COLS)
    col_off = kc[None, :] - qc[:, None] + (WIN_COLS - 1)
    col_sel = ((col_off[:, :, None] == jnp.arange(2 * WIN_COLS - 1)) & inside[:, :, None]).astype(f32)
    t = jnp.einsum('hrc,qkc->hrqk', rpb.astype(f32), col_sel, precision=lax.Precision.HIGHEST)
    t = jnp.where(inside[None, None], t * LOG2E, NEG)
    t = jnp.stack([t[:, WIN_ROWS - 1 - d:2 * WIN_ROWS - 1 - d] for d in range(WIN_ROWS)], axis=1)
    t = jnp.transpose(t, (1, 2, 4, 0, 3))
    t = t.reshape(WIN_ROWS, WIN_ROWS * GRID_W, HEAD_PAIRS, 2 * GRID_W)
    return jnp.transpose(t, (2, 0, 1, 3))


def _na_attention(qkv, kv_ctx, bias_t):
    bsz, seq, _ = qkv.shape
    ctx_len = kv_ctx.shape[1]
    rows = seq // GRID_W
    chunk_rows = min(NA_CHUNK_ROWS, rows)
    blk = chunk_rows * GRID_W
    halo = NA_HALO_ROWS * GRID_W
    n_chunks = seq // blk
    per = blk // halo
    n_halo = seq // halo
    hp2 = 2 * HEAD_DIM
    kblk = D // hp2
    vblk = 2 * D // hp2

    def cur(col0):
        return pl.BlockSpec((None, blk, hp2), lambda b, h, c: (b, c, col0 + h))

    def before(col0):
        return pl.BlockSpec((None, halo, hp2), lambda b, h, c: (b, jnp.maximum(c * per - 1, 0), col0 + h))

    def after(col0):
        return pl.BlockSpec((None, halo, hp2), lambda b, h, c: (b, jnp.minimum((c + 1) * per, n_halo - 1), col0 + h))

    return pl.pallas_call(
        functools.partial(_na_kernel, rows=rows, chunk_rows=chunk_rows),
        grid=(bsz, HEAD_PAIRS, n_chunks),
        in_specs=[cur(0),
                  before(kblk), cur(kblk), after(kblk),
                  before(vblk), cur(vblk), after(vblk),
                  pl.BlockSpec((None, ctx_len, hp2), lambda b, h, c: (b, 0, kblk + h)),
                  pl.BlockSpec((None, ctx_len, hp2), lambda b, h, c: (b, 0, vblk + h)),
                  pl.BlockSpec((None, WIN_ROWS, WIN_ROWS * GRID_W, hp2), lambda b, h, c: (h, 0, 0, 0))],
        out_specs=pl.BlockSpec((None, blk, hp2), lambda b, h, c: (b, c, h)),
        out_shape=jax.ShapeDtypeStruct((bsz, seq, D), bf16),
        scratch_shapes=[pltpu.VMEM((blk + 2 * halo, hp2), bf16), pltpu.VMEM((blk + 2 * halo, 2 * hp2), bf16),
                        pltpu.VMEM((ctx_len, 2 * hp2), bf16)],
        compiler_params=_params("parallel", "parallel", "arbitrary"),
        name="na_attention",
    )(qkv, qkv, qkv, qkv, qkv, qkv, qkv, kv_ctx, kv_ctx, bias_t)


def _mixer_epilogue(y, x_ref, gate_ref, g_post_ref, g_pre_ref, sh_ref, sc_ref, wr_ref, br_ref,
                    x1_ref, h2_ref, ti_ref, tg_ref):
    x1 = x_ref[...] + gate_ref[...] * _rms(y, g_post_ref[...])
    x1_ref[...] = x1
    h2 = _rms(x1, g_pre_ref[...]) * (1.0 + sc_ref[...]) + sh_ref[...]
    h2_ref[...] = h2.reshape(h2_ref.shape)
    hb = h2.astype(bf16)
    logits = lax.dot_general(wr_ref[...], hb, (((0,), (1,)), ((), ())), preferred_element_type=f32) + br_ref[...]
    eidx = lax.broadcasted_iota(jnp.int32, logits.shape, 0).astype(f32)
    work = logits
    vals, idxs = [], []
    for _ in range(TOP_K):
        m = jnp.max(work, axis=0, keepdims=True)
        first = jnp.min(jnp.where(work == m, eidx, float(N_EXPERTS)), axis=0, keepdims=True)
        vals.append(m)
        idxs.append(first)
        work = jnp.where(eidx == first, NEG, work)
    top_v = jnp.concatenate(vals, axis=0)
    e = jnp.exp(top_v - vals[0])
    tg_ref[...] = e / jnp.sum(e, axis=0, keepdims=True)
    ti_ref[...] = jnp.concatenate(idxs, axis=0).astype(jnp.int32)


def _attn_out_kernel(a_ref, wo_ref, *rest):
    y = jnp.dot(a_ref[...], wo_ref[...], preferred_element_type=f32)
    _mixer_epilogue(y, *rest)


def _epilogue_specs(tm):
    tok = lambda b, t: (b, t, 0)
    per_b = lambda b, t: (b, 0, 0)
    const = lambda b, t: (0, 0)
    in_specs = [pl.BlockSpec((None, tm, D), tok),
                pl.BlockSpec((None, 1, D), per_b),
                pl.BlockSpec((1, D), const),
                pl.BlockSpec((1, D), const),
                pl.BlockSpec((None, 1, D), per_b),
                pl.BlockSpec((None, 1, D), per_b),
                pl.BlockSpec((D, N_EXPERTS), const),
                pl.BlockSpec((N_EXPERTS, 1), const)]
    out_specs = [pl.BlockSpec((None, tm, D), tok),
                 pl.BlockSpec((None, tm, SUB, LANE), lambda b, t: (b, t, 0, 0)),
                 pl.BlockSpec((None, TOP_K, tm), lambda b, t: (b, 0, t)),
                 pl.BlockSpec((None, TOP_K, tm), lambda b, t: (b, 0, t))]
    return in_specs, out_specs


def _epilogue_shapes(bsz, seq):
    return [jax.ShapeDtypeStruct((bsz, seq, D), f32),
            jax.ShapeDtypeStruct((bsz, seq, SUB, LANE), f32),
            jax.ShapeDtypeStruct((bsz, TOP_K, seq), jnp.int32),
            jax.ShapeDtypeStruct((bsz, TOP_K, seq), f32)]


def _attn_out(a, wo, epi_args, *, tm):
    bsz, seq, _ = a.shape
    in_specs, out_specs = _epilogue_specs(tm)
    return pl.pallas_call(
        _attn_out_kernel,
        grid=(bsz, seq // tm),
        in_specs=[pl.BlockSpec((None, tm, D), lambda b, t: (b, t, 0)),
                  pl.BlockSpec((D, D), lambda b, t: (0, 0))] + in_specs,
        out_specs=out_specs,
        out_shape=_epilogue_shapes(bsz, seq),
        compiler_params=_params("parallel", "parallel"),
        name="attn_out",
    )(a, wo, *epi_args)


def _conv_kernel(up_ref, uc_ref, un_ref, wdw_ref, bdw_ref, lg_ref, lb_ref, w2_ref, b2_ref, *rest, tm, sub):
    ext, shifted = rest[-2:]
    t = pl.program_id(1)
    nt = pl.num_programs(1)
    ext[0:CONV_HALO] = jnp.where(t > 0, up_ref[...], 0.0)
    ext[CONV_HALO:CONV_HALO + tm] = uc_ref[...]
    ext[CONV_HALO + tm:] = jnp.where(t < nt - 1, un_ref[...], 0.0)
    span = tm + 2 * CONV_HALO - SUB
    for a in range(1, SUB):
        shifted[a - 1, 0:span] = ext[a:a + span]
    base = CONV_HALO - CONV_W // 2
    parts = []
    for rc in range(tm // sub):
        acc = jnp.zeros((sub, D), f32) + bdw_ref[...]
        for k in range(CONV_W):
            a, r0 = (base + k) % SUB, rc * sub + (base + k) // SUB * SUB
            win = ext[r0:r0 + sub, :] if a == 0 else shifted[a - 1, r0:r0 + sub, :]
            acc = acc + win * wdw_ref[k:k + 1, :]
        mu = jnp.mean(acc, axis=-1, keepdims=True)
        d = acc - mu
        var = jnp.mean(d * d, axis=-1, keepdims=True)
        z = d * lax.rsqrt(var + EPS) * lg_ref[...] + lb_ref[...]
        parts.append((z * jax.nn.sigmoid(z)).astype(bf16))
    z = jnp.concatenate(parts, axis=0)
    y = jnp.dot(z, w2_ref[...], preferred_element_type=f32) + b2_ref[...]
    _mixer_epilogue(y, *rest[:-2])


def _conv_tail(u, w_dw, b_dw, ln_g, ln_b, w2, b2, epi_args, *, tm):
    bsz, seq, _ = u.shape
    in_specs, out_specs = _epilogue_specs(tm)
    per = tm // CONV_HALO
    n_halo = seq // CONV_HALO
    const = lambda b, t: (0, 0)
    return pl.pallas_call(
        functools.partial(_conv_kernel, tm=tm, sub=8),
        grid=(bsz, seq // tm),
        in_specs=[pl.BlockSpec((None, CONV_HALO, D), lambda b, t: (b, jnp.maximum(t * per - 1, 0), 0)),
                  pl.BlockSpec((None, tm, D), lambda b, t: (b, t, 0)),
                  pl.BlockSpec((None, CONV_HALO, D), lambda b, t: (b, jnp.minimum((t + 1) * per, n_halo - 1), 0)),
                  pl.BlockSpec((CONV_W, D), const),
                  pl.BlockSpec((1, D), const),
                  pl.BlockSpec((1, D), const),
                  pl.BlockSpec((1, D), const),
                  pl.BlockSpec((D, D), const),
                  pl.BlockSpec((1, D), const)] + in_specs,
        out_specs=out_specs,
        out_shape=_epilogue_shapes(bsz, seq),
        scratch_shapes=[pltpu.VMEM((tm + 2 * CONV_HALO, D), f32),
                        pltpu.VMEM((SUB - 1, tm + 2 * CONV_HALO, D), f32)],
        compiler_params=_params("parallel", "arbitrary"),
        name="conv_tail",
    )(u, u, u, w_dw, b_dw, ln_g, ln_b, w2, b2, *epi_args)


def _row_copies(src, src_off, dst, dst_off, n, sem, max_bit, fixed_src=False):
    out = []
    for b in range(max_bit, -1, -1):
        size = SUB << b
        start = (n >> (b + 1)) << (b + 1)
        s_off = src_off if fixed_src else src_off + start
        cp = pltpu.make_async_copy(src.at[pl.ds(pl.multiple_of(s_off * SUB, SUB), size)],
                                   dst.at[pl.ds(pl.multiple_of((dst_off + start) * SUB, SUB), size)], sem)
        out.append((((n >> b) & 1) == 1, cp))
    return out


def _start_all(copies):
    for cond, cp in copies:
        pl.when(cond)(cp.start)


def _wait_all(copies):
    for cond, cp in copies:
        pl.when(cond)(cp.wait)


def _dispatch_kernel(nseg_ref, pos_ref, off_ref, padn_ref, padpos_ref, slots_hbm, x_ref, xs_hbm,
                     slot_s, stage, zeros, sem_s, sem):
    i = pl.program_id(0)
    last = pl.num_programs(0) - 1
    buf = i % 2
    t4 = TOP_K * MOE_TILE
    cp = pltpu.make_async_copy(slots_hbm.at[pl.ds(i * t4, t4)], slot_s, sem_s)
    cp.start()
    cp.wait()
    stage_b = stage.at[buf]

    def tok(tb, carry):
        for u in range(ROW_UNROLL):
            t = tb * ROW_UNROLL + u
            v = x_ref[t]
            for k in range(TOP_K):
                stage_b[pl.ds(pl.multiple_of(slot_s[t * TOP_K + k], SUB), SUB), :] = v
        return carry

    lax.fori_loop(0, MOE_TILE // ROW_UNROLL, tok, 0)

    def seg(tile, b, e):
        j = tile * N_EXPERTS + e
        return _row_copies(stage.at[b], off_ref[j], xs_hbm, pos_ref[j], nseg_ref[j], sem.at[b], MOE_TILE_BITS)

    def for_experts(fn):
        def body(e, carry):
            fn(e)
            return carry
        lax.fori_loop(0, N_EXPERTS, body, 0)

    for_experts(lambda e: _start_all(seg(i, buf, e)))

    @pl.when(i > 0)
    def _():
        for_experts(lambda e: _wait_all(seg(i - 1, 1 - buf, e)))

    @pl.when(i == last)
    def _():
        for_experts(lambda e: _wait_all(seg(i, buf, e)))
        zeros[...] = jnp.zeros_like(zeros)
        zsem = sem.at[0]

        def pad(e):
            return _row_copies(zeros, 0, xs_hbm, padpos_ref[e], padn_ref[e], zsem, PAD_BITS, fixed_src=True)

        for_experts(lambda e: _start_all(pad(e)))
        for_experts(lambda e: _wait_all(pad(e)))

        tail0 = padpos_ref[N_EXPERTS] * SUB
        zrows = zeros.shape[0]

        def tail(j):
            return pltpu.make_async_copy(zeros, xs_hbm.at[pl.ds(pl.multiple_of(tail0 + j * zrows, SUB), zrows)], zsem)

        def tail_start(j, carry):
            tail(j).start()
            return carry

        def tail_wait(j, carry):
            tail(j).wait()
            return carry

        n_tail = (xs_hbm.shape[0] - tail0) // zrows
        lax.fori_loop(0, n_tail, tail_start, 0)
        lax.fori_loop(0, n_tail, tail_wait, 0)


def _dispatch(h2t, slots, nseg, pos, off, padn, padpos, n_rows):
    n_tok = h2t.shape[0]
    nt = n_tok // MOE_TILE
    grid_spec = pltpu.PrefetchScalarGridSpec(
        num_scalar_prefetch=5,
        grid=(nt,),
        in_specs=[pl.BlockSpec(memory_space=pl.ANY),
                  pl.BlockSpec((MOE_TILE, SUB, LANE), lambda i, *_: (i, 0, 0))],
        out_specs=pl.BlockSpec(memory_space=pl.ANY),
        scratch_shapes=[pltpu.SMEM((TOP_K * MOE_TILE,), jnp.int32),
                        pltpu.VMEM((2, TOP_K * MOE_TILE * SUB, LANE), f32),
                        pltpu.VMEM((SUB << PAD_BITS, LANE), f32),
                        pltpu.SemaphoreType.DMA(()),
                        pltpu.SemaphoreType.DMA((2,))],
    )
    return pl.pallas_call(
        _dispatch_kernel,
        grid_spec=grid_spec,
        out_shape=jax.ShapeDtypeStruct((n_rows * SUB, LANE), f32),
        compiler_params=_params("arbitrary"),
        name="moe_dispatch",
    )(nseg, pos, off, padn, padpos, slots, h2t)


def _experts_kernel(be_ref, bv_ref, bf_ref, xs_ref, wgu_ref, bgu_ref, wd_ref, bd_ref, y_ref, wgu_s, wd_s):
    b = pl.program_id(0)

    @pl.when(bf_ref[b] == 1)
    def _():
        wgu_s[...] = wgu_ref[...].astype(bf16)
        wd_s[...] = wd_ref[...].astype(bf16)

    @pl.when(bv_ref[b] == 1)
    def _():
        x = xs_ref[...].reshape(EXPERT_BLOCK, D).astype(bf16)
        gu = jnp.dot(x, wgu_s[...], preferred_element_type=f32) + bgu_ref[...]
        gate = jnp.minimum(gu[:, :D], SWIGLU_LIMIT)
        up = jnp.clip(gu[:, D:], -SWIGLU_LIMIT, SWIGLU_LIMIT)
        glu = gate * jax.nn.sigmoid(gate * SWIGLU_ALPHA)
        a = ((up + 1.0) * glu).astype(bf16)
        y = jnp.dot(a, wd_s[...], preferred_element_type=f32) + bd_ref[...]
        y_ref[...] = y.reshape(EXPERT_BLOCK, SUB, LANE)

    @pl.when(bv_ref[b] == 0)
    def _():
        y_ref[...] = jnp.zeros_like(y_ref)


def _experts(xs, blk_e, blk_valid, blk_first, w_gu, b_gu, w_down, b_down, layer):
    n_rows = xs.shape[0]
    n_blocks = n_rows // EXPERT_BLOCK
    grid_spec = pltpu.PrefetchScalarGridSpec(
        num_scalar_prefetch=3,
        grid=(n_blocks,),
        in_specs=[pl.BlockSpec((EXPERT_BLOCK, SUB, LANE), lambda b, be, bv, bf: (b, 0, 0)),
                  pl.BlockSpec((None, None, D, 2 * D), lambda b, be, bv, bf: (layer, be[b], 0, 0)),
                  pl.BlockSpec((None, None, 1, 2 * D), lambda b, be, bv, bf: (layer, be[b], 0, 0)),
                  pl.BlockSpec((None, None, D, D), lambda b, be, bv, bf: (layer, be[b], 0, 0)),
                  pl.BlockSpec((None, None, 1, D), lambda b, be, bv, bf: (layer, be[b], 0, 0))],
        out_specs=pl.BlockSpec((EXPERT_BLOCK, SUB, LANE), lambda b, be, bv, bf: (b, 0, 0)),
        scratch_shapes=[pltpu.VMEM((D, 2 * D), bf16), pltpu.VMEM((D, D), bf16)],
    )
    return pl.pallas_call(
        _experts_kernel,
        grid_spec=grid_spec,
        out_shape=jax.ShapeDtypeStruct((n_rows, SUB, LANE), f32),
        compiler_params=_params("arbitrary"),
        name="experts",
    )(blk_e, blk_valid, blk_first, xs, w_gu, b_gu[:, :, None, :], w_down, b_down[:, :, None, :])


def _combine_kernel(nseg_ref, pos_ref, off_ref, slots_hbm, gates_hbm, ys_hbm, x1_ref, gate_ref, g_ref, o_ref,
                    slot_s, gate_s, ybuf, mbuf, sem_s, sem):
    i = pl.program_id(0)
    last = pl.num_programs(0) - 1
    buf = i % 2
    t4 = TOP_K * MOE_TILE
    cps = pltpu.make_async_copy(slots_hbm.at[pl.ds(i * t4, t4)], slot_s, sem_s.at[0])
    cpg = pltpu.make_async_copy(gates_hbm.at[pl.ds(i * t4, t4)], gate_s, sem_s.at[1])
    cps.start()
    cpg.start()

    def seg(tile, b, e):
        j = tile * N_EXPERTS + e
        return _row_copies(ys_hbm, pos_ref[j], ybuf.at[b], off_ref[j], nseg_ref[j], sem.at[b], MOE_TILE_BITS)

    def for_experts(fn):
        def body(e, carry):
            fn(e)
            return carry
        lax.fori_loop(0, N_EXPERTS, body, 0)

    @pl.when(i == 0)
    def _():
        for_experts(lambda e: _start_all(seg(i, buf, e)))

    @pl.when(i < last)
    def _():
        for_experts(lambda e: _start_all(seg(i + 1, 1 - buf, e)))

    cps.wait()
    cpg.wait()
    for_experts(lambda e: _wait_all(seg(i, buf, e)))
    ybuf_b = ybuf.at[buf]

    def row(j):
        return gate_s[j] * ybuf_b[pl.ds(pl.multiple_of(slot_s[j], SUB), SUB), :]

    def chunk(ci, carry):
        def tok(tb, carry2):
            for u in range(ROW_UNROLL):
                t = tb * ROW_UNROLL + u
                j = (ci * COMBINE_CHUNK + t) * TOP_K
                acc = row(j)
                for k in range(1, TOP_K):
                    acc = acc + row(j + k)
                mbuf[t] = acc
            return carry2

        lax.fori_loop(0, COMBINE_CHUNK // ROW_UNROLL, tok, 0)
        r0 = pl.multiple_of(ci * COMBINE_CHUNK, COMBINE_CHUNK)
        m = mbuf[...].reshape(COMBINE_CHUNK, D)
        o_ref[pl.ds(r0, COMBINE_CHUNK), :] = (x1_ref[pl.ds(r0, COMBINE_CHUNK), :]
                                              + gate_ref[...] * _rms(m, g_ref[...]))
        return carry

    lax.fori_loop(0, MOE_TILE // COMBINE_CHUNK, chunk, 0)


def _combine(ys, slots, gates, nseg, pos, off, x1, gate, g, tiles_per_batch):
    n_tok = x1.shape[0]
    nt = n_tok // MOE_TILE
    grid_spec = pltpu.PrefetchScalarGridSpec(
        num_scalar_prefetch=3,
        grid=(nt,),
        in_specs=[pl.BlockSpec(memory_space=pl.ANY),
                  pl.BlockSpec(memory_space=pl.ANY),
                  pl.BlockSpec(memory_space=pl.ANY),
                  pl.BlockSpec((MOE_TILE, D), lambda i, *_: (i, 0)),
                  pl.BlockSpec((None, 1, D), lambda i, *_: (i // tiles_per_batch, 0, 0)),
                  pl.BlockSpec((1, D), lambda i, *_: (0, 0))],
        out_specs=pl.BlockSpec((MOE_TILE, D), lambda i, *_: (i, 0)),
        scratch_shapes=[pltpu.SMEM((TOP_K * MOE_TILE,), jnp.int32),
                        pltpu.SMEM((TOP_K * MOE_TILE,), f32),
                        pltpu.VMEM((2, TOP_K * MOE_TILE * SUB, LANE), f32),
                        pltpu.VMEM((COMBINE_CHUNK, SUB, LANE), f32),
                        pltpu.SemaphoreType.DMA((2,)),
                        pltpu.SemaphoreType.DMA((2,))],
    )
    return pl.pallas_call(
        _combine_kernel,
        grid_spec=grid_spec,
        out_shape=jax.ShapeDtypeStruct((n_tok, D), f32),
        compiler_params=_params("arbitrary"),
        name="moe_combine",
    )(nseg, pos, off, slots, gates, ys, x1, gate, g)


def _moe_tables(top_i):
    n_tok = top_i.shape[0]
    nt = n_tok // MOE_TILE
    oh4 = (top_i[:, :, None] == jnp.arange(N_EXPERTS)[None, None, :]).astype(f32)
    oh = oh4.sum(axis=1).reshape(nt, MOE_TILE, N_EXPERTS)
    tri = (jnp.arange(MOE_TILE)[:, None] > jnp.arange(MOE_TILE)[None, :]).astype(bf16)
    rank = jnp.einsum('ab,nbe->nae', tri, oh.astype(bf16), preferred_element_type=f32)
    nseg = oh.sum(axis=1).astype(jnp.int32)
    counts = nseg.sum(axis=0)
    pcounts = (counts + EXPERT_BLOCK - 1) // EXPERT_BLOCK * EXPERT_BLOCK
    pend = jnp.cumsum(pcounts)
    pstart = pend - pcounts
    pos = pstart[None, :] + jnp.cumsum(nseg, axis=0) - nseg
    off = jnp.cumsum(nseg, axis=1) - nseg
    base = (off.astype(f32)[:, None, :] + rank).reshape(n_tok, 1, N_EXPERTS)
    slots = (jnp.sum(oh4 * base, axis=-1).astype(jnp.int32) * SUB).reshape(-1)
    n_blocks = -(-(n_tok * TOP_K) // EXPERT_BLOCK) + N_EXPERTS
    blk_start = jnp.arange(n_blocks, dtype=jnp.int32) * EXPERT_BLOCK
    blk_e = jnp.minimum(jnp.sum(blk_start[:, None] >= pend[None, :], axis=1), N_EXPERTS - 1).astype(jnp.int32)
    blk_valid = (blk_start < pend[-1]).astype(jnp.int32)
    blk_first = jnp.concatenate([jnp.ones((1,), jnp.int32), (blk_e[1:] != blk_e[:-1]).astype(jnp.int32)])
    tabs = dict(nseg=nseg.reshape(-1), pos=pos.reshape(-1).astype(jnp.int32), off=off.reshape(-1),
                padn=(pcounts - counts).astype(jnp.int32),
                padpos=jnp.concatenate([pstart + counts, pend[-1:]]).astype(jnp.int32),
                slots=slots, blk_e=blk_e, blk_valid=blk_valid, blk_first=blk_first)
    return tabs, n_blocks * EXPERT_BLOCK


def _moe_residual(h2t, top_i, top_g, x1, gate, g, w_gu, b_gu, w_down, b_down, layer, tiles_per_batch):
    t, n_rows = _moe_tables(top_i)
    xs = _dispatch(h2t, t["slots"], t["nseg"], t["pos"], t["off"], t["padn"], t["padpos"], n_rows)
    ys = _experts(xs.reshape(n_rows, SUB, LANE), t["blk_e"], t["blk_valid"], t["blk_first"],
                  w_gu, b_gu, w_down, b_down, layer)
    return _combine(ys.reshape(n_rows * SUB, LANE), t["slots"], top_g.reshape(-1), t["nseg"], t["pos"], t["off"], x1, gate, g,
                    tiles_per_batch)


def kernel(x, c, ctx, c_ctx, w_ada, b_ada, norm_g, w_qkv, w_o, rpb, w_pw1, b_pw1, w_dw, b_dw, cn_g, cn_b,
           w_pw2, b_pw2, w_router, b_router, w_gu, b_gu, w_down, b_down):
    bsz, seq, _ = x.shape
    tm = min(512, seq)
    cvec = jnp.zeros((8, D), f32).at[:bsz].set(c).at[bsz].set(c_ctx)
    mod = _ada(cvec, w_ada, b_ada)

    def lat(i, k):
        return mod[i, :bsz, k * D:(k + 1) * D][:, None, :]

    def cx(i, k):
        return jnp.broadcast_to(mod[i, bsz, k * D:(k + 1) * D][None, None, :], (bsz, 1, D))

    row = lambda v: v.reshape(1, -1)
    zero_b = jnp.zeros((1, 3 * D), f32)

    wq = w_qkv[0].astype(bf16)
    qkv = _prenorm_matmul(x, row(norm_g[0, 0]), lat(0, 0), lat(0, 1), wq, zero_b,
                          mode="qkv", tm=tm, out_dtype=bf16)
    kv_ctx = _prenorm_matmul(ctx, row(norm_g[0, 0]), cx(0, 0), cx(0, 1), wq, zero_b,
                             mode="qkv", tm=ctx.shape[1], out_dtype=bf16)
    attn = _na_attention(qkv, kv_ctx, _na_bias_table(rpb[0]))
    epi = (x, lat(0, 2), row(norm_g[0, 1]), row(norm_g[0, 2]), lat(0, 3), lat(0, 4),
           w_router[0].astype(bf16), b_router[0].reshape(-1, 1))
    x1, h2, ti, tg = _attn_out(attn, w_o[0].astype(bf16), epi, tm=tm)
    x = _moe_residual(h2.reshape(-1, SUB, LANE), ti.transpose(0, 2, 1).reshape(-1, TOP_K), tg.transpose(0, 2, 1).reshape(-1, TOP_K),
                      x1.reshape(-1, D), lat(0, 5), row(norm_g[0, 3]),
                      w_gu, b_gu, w_down, b_down, 0, seq // MOE_TILE).reshape(bsz, seq, D)

    u = _prenorm_matmul(x, row(norm_g[1, 0]), lat(1, 0), lat(1, 1), w_pw1[0].astype(bf16), row(b_pw1[0]),
                        mode="glu", tm=tm, out_dtype=f32)
    epi = (x, lat(1, 2), row(norm_g[1, 1]), row(norm_g[1, 2]), lat(1, 3), lat(1, 4),
           w_router[1].astype(bf16), b_router[1].reshape(-1, 1))
    x1, h2, ti, tg = _conv_tail(u, w_dw[0], row(b_dw[0]), row(cn_g[0]), row(cn_b[0]),
                                w_pw2[0].astype(bf16), row(b_pw2[0]), epi, tm=min(512, seq))
    return _moe_residual(h2.reshape(-1, SUB, LANE), ti.transpose(0, 2, 1).reshape(-1, TOP_K), tg.transpose(0, 2, 1).reshape(-1, TOP_K),
                         x1.reshape(-1, D), lat(1, 5), row(norm_g[1, 3]),
                         w_gu, b_gu, w_down, b_down, 1, seq // MOE_TILE).reshape(bsz, seq, D)
```
